```python
import jax, jax.numpy as jnp
from jax import lax
import numpy as np

D_MODEL = 1024
BATCH = 4
SEQ = 4096
DEPTH = 2
DEC_BATCH = 128
DEC_SEQ = 4
PAST_LEN = 2048
PAGE_SIZE = 128

HEAD_DIM = 64
N_HEADS_A = 6
N_KV_A = 2
N_HEADS_B = 6
N_IDX_HEADS = 4
IDX_DIM = 64
TOPK_MAX = 256
POOL_WINDOWS = (2, 4, 8, 16)
N_POOL_GROUPS = 4
POOL_GROUP_DIM = 64
POOL_STATE = 15
WIDTH_A = N_HEADS_A * HEAD_DIM
WIDTH_KV_A = N_KV_A * HEAD_DIM
WIDTH_B = N_HEADS_B * HEAD_DIM
WIDTH_C = N_POOL_GROUPS * POOL_GROUP_DIM
MIX_WIDTH = WIDTH_A + WIDTH_B + WIDTH_C
IN_SPLITS = (WIDTH_A, WIDTH_KV_A, WIDTH_KV_A, N_IDX_HEADS * IDX_DIM, IDX_DIM, N_IDX_HEADS, WIDTH_B, WIDTH_B, WIDTH_B, WIDTH_C)
IN_WIDTH = sum(IN_SPLITS)
D_FF = 2816
CONV_WIDTH = 3
PLE_DIM = 256
ROPE_THETA = 10000.0
Q_BLOCK = 128
EPS = 1e-6

kernel_name = 'hybrid_dsa_stickbreak_pool_decoder_step'


def rmsnorm(x, g):
    xf = x.astype(jnp.float32)
    r = lax.rsqrt(jnp.mean(xf * xf, axis=-1, keepdims=True) + EPS)
    return (xf * r).astype(x.dtype) * g


def rope(x, pos):
    half = x.shape[-1] // 2
    inv = ROPE_THETA ** (-jnp.arange(half, dtype=jnp.float32) / half)
    ang = pos.astype(jnp.float32)[:, None] * inv[None, :]
    cos = jnp.cos(ang)[:, None, :]
    sin = jnp.sin(ang)[:, None, :]
    xf = x.astype(jnp.float32)
    x1, x2 = xf[..., :half], xf[..., half:]
    return jnp.concatenate([x1 * cos - x2 * sin, x2 * cos + x1 * sin], axis=-1).astype(x.dtype)


def split_columns(proj):
    out, start = [], 0
    for w in IN_SPLITS:
        out.append(proj[..., start:start + w])
        start += w
    return out


def gather_pages(pool, page_table):
    g = pool[page_table]
    return g.reshape(g.shape[0], g.shape[1] * g.shape[2], *g.shape[3:])


def sweep_query_blocks(fn, q_pos, *q_arrays):
    T = q_pos.shape[0]
    qb = min(Q_BLOCK, T)
    nb = T // qb
    blocks = tuple(jnp.moveaxis(a.reshape(a.shape[0], nb, qb, *a.shape[2:]), 1, 0) for a in q_arrays)
    out = lax.map(lambda args: fn(*args), (q_pos.reshape(nb, qb),) + blocks)
    out = jnp.moveaxis(out, 0, 1)
    return out.reshape(out.shape[0], T, *out.shape[3:])


def dsa_block(q_pos, q, qi, wi, k, v, ki, k_pos, topk):
    s_idx = jnp.einsum('bqhd,bsd->bqhs', qi, ki).astype(jnp.float32) * IDX_DIM ** -0.5
    score = jnp.einsum('bqhs,bqh->bqs', jax.nn.relu(s_idx), wi.astype(jnp.float32))
    causal = k_pos[None, :] <= q_pos[:, None]
    score = jnp.where(causal[None], score, -jnp.inf)
    _, sel = lax.top_k(score, topk)
    k_sel = jax.vmap(lambda kb, ib: kb[ib])(k, sel)
    v_sel = jax.vmap(lambda vb, ib: vb[ib])(v, sel)
    valid = k_pos[sel] <= q_pos[None, :, None]
    B, Tq, HA, dh = q.shape
    qg = q.reshape(B, Tq, N_KV_A, HA // N_KV_A, dh)
    logits = jnp.einsum('bqngd,bqknd->bqngk', qg, k_sel).astype(jnp.float32) * dh ** -0.5
    logits = jnp.where(valid[:, :, None, None, :], logits, -jnp.inf)
    attn = jax.nn.softmax(logits, axis=-1).astype(v.dtype)
    o = jnp.einsum('bqngk,bqknd->bqngd', attn, v_sel)
    return o.reshape(B, Tq, HA * dh)


def stick_breaking_block(q_pos, q, k, v, k_pos):
    z = jnp.einsum('bqhd,bshd->bhqs', q, k).astype(jnp.float32) * q.shape[-1] ** -0.5
    strict = (k_pos[None, :] < q_pos[:, None])[None, None]
    log_rem = jnp.where(strict, jax.nn.log_sigmoid(-z), 0.0)
    rev = lax.cumsum(log_rem, axis=3, reverse=True)
    log_a = jnp.where(strict, z + rev, -jnp.inf)
    a = jnp.exp(log_a).astype(v.dtype)
    return jnp.einsum('bhqs,bshd->bqhd', a, v)


def pool_mix(u_ext, pos, w_pool, pool_scale):
    B = u_ext.shape[0]
    T = u_ext.shape[1] - POOL_STATE
    uf = u_ext.astype(jnp.float32)
    cs = jnp.concatenate([jnp.zeros((B, 1, WIDTH_C), jnp.float32), jnp.cumsum(uf, axis=1)], axis=1)
    u = uf[:, POOL_STATE:]
    means = []
    for g, w in enumerate(POOL_WINDOWS):
        c0, c1 = g * POOL_GROUP_DIM, (g + 1) * POOL_GROUP_DIM
        upper = cs[:, POOL_STATE + 1:, c0:c1]
        lower = cs[:, POOL_STATE + 1 - w:POOL_STATE + 1 - w + T, c0:c1]
        cnt = jnp.minimum(w, pos + 1).astype(jnp.float32)[None, :, None]
        means.append((upper - lower) / cnt)
    pooled = (jnp.concatenate(means, axis=-1) - u).astype(u_ext.dtype)
    pg = pooled.reshape(B, T, N_POOL_GROUPS, POOL_GROUP_DIM)
    y = jnp.einsum('btgc,gcd->btgd', pg, w_pool).reshape(B, T, WIDTH_C)
    return y * pool_scale


def conv_ffn(f, gate_prefix, w_up, conv_w, conv_b, w_down):
    up = f @ w_up
    gate, val = up[..., :D_FF], up[..., D_FF:]
    g_ext = jnp.concatenate([gate_prefix, gate], axis=1)
    T = gate.shape[1]
    conv = conv_b + g_ext[:, 0:T] * conv_w[0]
    for j in range(1, CONV_WIDTH):
        conv = conv + g_ext[:, j:j + T] * conv_w[j]
    y = (jax.nn.silu(conv) * val) @ w_down
    return y, g_ext[:, -(CONV_WIDTH - 1):]


def trunk_layer(h, p, pos0, past, weights):
    (g_attn, w_in, g_q_a, g_k_a, w_pool, pool_scale, w_out,
     g_ffn, w_up, conv_w, conv_b, w_down, g_ple, w_ple_gate, w_ple_proj) = weights
    B, T, _ = h.shape
    pos = pos0 + jnp.arange(T, dtype=jnp.int32)
    a = rmsnorm(h, g_attn)
    qa, ka, va, qi, ki, wi, qb, kb, vb, u = split_columns(a @ w_in)
    qa = rope(rmsnorm(qa.reshape(B, T, N_HEADS_A, HEAD_DIM), g_q_a), pos)
    ka = rope(rmsnorm(ka.reshape(B, T, N_KV_A, HEAD_DIM), g_k_a), pos)
    va = va.reshape(B, T, N_KV_A, HEAD_DIM)
    qi = rope(qi.reshape(B, T, N_IDX_HEADS, IDX_DIM), pos)
    ki = rope(ki[:, :, None, :], pos)[:, :, 0]
    wi = wi * N_IDX_HEADS ** -0.5
    qb = qb.reshape(B, T, N_HEADS_B, HEAD_DIM)
    kb = kb.reshape(B, T, N_HEADS_B, HEAD_DIM)
    vb = vb.reshape(B, T, N_HEADS_B, HEAD_DIM)
    if past is None:
        ka_f, va_f, ki_f, kb_f, vb_f = ka, va, ki, kb, vb
        pool_prefix = jnp.zeros((B, POOL_STATE, WIDTH_C), u.dtype)
        conv_prefix = jnp.zeros((B, CONV_WIDTH - 1, D_FF), h.dtype)
    else:
        pa_k, pa_v, pi_k, pb_k, pb_v, pool_prefix, conv_prefix = past
        ka_f = jnp.concatenate([pa_k, ka], axis=1)
        va_f = jnp.concatenate([pa_v, va], axis=1)
        ki_f = jnp.concatenate([pi_k, ki], axis=1)
        kb_f = jnp.concatenate([pb_k, kb], axis=1)
        vb_f = jnp.concatenate([pb_v, vb], axis=1)
    L = ka_f.shape[1]
    k_pos = jnp.arange(L, dtype=jnp.int32)
    topk = min(TOPK_MAX, L // 4)
    o_a = sweep_query_blocks(
        lambda qp, q_, qi_, wi_: dsa_block(qp, q_, qi_, wi_, ka_f, va_f, ki_f, k_pos, topk),
        pos, qa, qi, wi)
    o_b = sweep_query_blocks(
        lambda qp, q_: stick_breaking_block(qp, q_, kb_f, vb_f, k_pos), pos, qb)
    u_ext = jnp.concatenate([pool_prefix, u], axis=1)
    o_c = pool_mix(u_ext, pos, w_pool, pool_scale)
    mix = jnp.concatenate([o_a, o_b.reshape(B, T, WIDTH_B), o_c], axis=-1)
    h = h + mix @ w_out
    y_ffn, new_conv = conv_ffn(rmsnorm(h, g_ffn), conv_prefix, w_up, conv_w, conv_b, w_down)
    h = h + y_ffn
    h = h + jax.nn.sigmoid(rmsnorm(h, g_ple) @ w_ple_gate) * (p @ w_ple_proj)
    return h, (ka, va, ki, kb, vb, u_ext[:, -POOL_STATE:], new_conv)


def setup_inputs(seed: int = 0) -> dict:
    key = jax.random.key(seed)
    ks = jax.random.split(key, 32)
    n_pages = PAST_LEN // PAGE_SIZE
    n_pool = (DEC_BATCH * n_pages * 5) // 4

    def nrm(k, shape, s=1.0):
        return s * jax.random.normal(k, shape, jnp.float32)

    def gain(k, shape):
        return 1.0 + 0.05 * jax.random.normal(k, shape, jnp.float32)

    page_table = jax.random.permutation(ks[0], n_pool)[:DEC_BATCH * n_pages].reshape(DEC_BATCH, n_pages).astype(jnp.int32)
    return {
        'x_prompt': nrm(ks[1], (BATCH, SEQ, D_MODEL)),
        'x_sample': nrm(ks[2], (DEC_BATCH, DEC_SEQ, D_MODEL)),
        'cache_a_k': nrm(ks[3], (DEPTH, n_pool, PAGE_SIZE, N_KV_A, HEAD_DIM)),
        'cache_a_v': nrm(ks[4], (DEPTH, n_pool, PAGE_SIZE, N_KV_A, HEAD_DIM)),
        'cache_idx_k': nrm(ks[5], (DEPTH, n_pool, PAGE_SIZE, IDX_DIM)),
        'cache_b_k': nrm(ks[6], (DEPTH, n_pool, PAGE_SIZE, N_HEADS_B, HEAD_DIM)),
        'cache_b_v': nrm(ks[7], (DEPTH, n_pool, PAGE_SIZE, N_HEADS_B, HEAD_DIM)),
        'state_pool': nrm(ks[8], (DEPTH, DEC_BATCH, POOL_STATE, WIDTH_C)),
        'state_conv': nrm(ks[9], (DEPTH, DEC_BATCH, CONV_WIDTH - 1, D_FF)),
        'page_table': page_table,
        'p_prompt': nrm(ks[10], (DEPTH, BATCH, SEQ, PLE_DIM)),
        'p_sample': nrm(ks[11], (DEPTH, DEC_BATCH, DEC_SEQ, PLE_DIM)),
        'g_attn': gain(ks[12], (DEPTH, D_MODEL)),
        'w_in': nrm(ks[13], (DEPTH, D_MODEL, IN_WIDTH), D_MODEL ** -0.5),
        'g_q_a': gain(ks[14], (DEPTH, HEAD_DIM)),
        'g_k_a': gain(ks[15], (DEPTH, HEAD_DIM)),
        'w_pool': nrm(ks[16], (DEPTH, N_POOL_GROUPS, POOL_GROUP_DIM, POOL_GROUP_DIM), POOL_GROUP_DIM ** -0.5),
        'pool_scale': gain(ks[17], (DEPTH, WIDTH_C)),
        'w_out': nrm(ks[18], (DEPTH, MIX_WIDTH, D_MODEL), MIX_WIDTH ** -0.5),
        'g_ffn': gain(ks[19], (DEPTH, D_MODEL)),
        'w_up': nrm(ks[20], (DEPTH, D_MODEL, 2 * D_FF), D_MODEL ** -0.5),
        'conv_w': nrm(ks[21], (DEPTH, CONV_WIDTH, D_FF), CONV_WIDTH ** -0.5),
        'conv_b': nrm(ks[22], (DEPTH, D_FF), 0.02),
        'w_down': nrm(ks[23], (DEPTH, D_FF, D_MODEL), D_FF ** -0.5),
        'g_ple': gain(ks[24], (DEPTH, D_MODEL)),
        'w_ple_gate': nrm(ks[25], (DEPTH, D_MODEL, D_MODEL), D_MODEL ** -0.5),
        'w_ple_proj': nrm(ks[26], (DEPTH, PLE_DIM, D_MODEL), PLE_DIM ** -0.5),
    }


def reference(x_prompt, x_sample, cache_a_k, cache_a_v, cache_idx_k, cache_b_k, cache_b_v,
              state_pool, state_conv, page_table, p_prompt, p_sample,
              g_attn, w_in, g_q_a, g_k_a, w_pool, pool_scale, w_out,
              g_ffn, w_up, conv_w, conv_b, w_down, g_ple, w_ple_gate, w_ple_proj):
    h_p = x_prompt
    h_s = x_sample
    rows_p = []
    rows_s = []
    for l in range(DEPTH):
        weights = (g_attn[l], w_in[l], g_q_a[l], g_k_a[l], w_pool[l], pool_scale[l], w_out[l],
                   g_ffn[l], w_up[l], conv_w[l], conv_b[l], w_down[l], g_ple[l], w_ple_gate[l], w_ple_proj[l])
        h_p, st_p = trunk_layer(h_p, p_prompt[l], 0, None, weights)
        past = (gather_pages(cache_a_k[l], page_table), gather_pages(cache_a_v[l], page_table),
                gather_pages(cache_idx_k[l], page_table), gather_pages(cache_b_k[l], page_table),
                gather_pages(cache_b_v[l], page_table), state_pool[l], state_conv[l])
        h_s, st_s = trunk_layer(h_s, p_sample[l], PAST_LEN, past, weights)
        rows_p.append(st_p)
        rows_s.append(st_s)

    def stack(rows, i):
        return jnp.stack([r[i] for r in rows], axis=0)

    a_k_p, a_v_p, idx_k_p = stack(rows_p, 0), stack(rows_p, 1), stack(rows_p, 2)
    b_k_p, b_v_p = stack(rows_p, 3), stack(rows_p, 4)
    pool_p, conv_p = stack(rows_p, 5), stack(rows_p, 6)
    a_k_s, a_v_s, idx_k_s = stack(rows_s, 0), stack(rows_s, 1), stack(rows_s, 2)
    b_k_s, b_v_s = stack(rows_s, 3), stack(rows_s, 4)
    pool_s, conv_s = stack(rows_s, 5), stack(rows_s, 6)
    return (h_p, h_s, a_k_p, a_v_p, idx_k_p, b_k_p, b_v_p, pool_p, conv_p,
            a_k_s, a_v_s, idx_k_s, b_k_s, b_v_s, pool_s, conv_s)
```

```python
import functools

import jax
import jax.numpy as jnp
from jax import lax
from jax.experimental import pallas as pl
from jax.experimental.pallas import tpu as pltpu

F32 = jnp.float32
BF16 = jnp.bfloat16
I32 = jnp.int32

HEAD_DIM = 64
N_HEADS_A = 6
N_KV_A = 2
N_HEADS_B = 6
N_IDX_HEADS = 4
IDX_DIM = 64
TOPK_MAX = 256
POOL_WINDOWS = (2, 4, 8, 16)
POOL_GROUP_DIM = 64
POOL_STATE = 15
CONV_WIDTH = 3
ROPE_THETA = 10000.0
EPS = 1e-6
WIDTH_A = N_HEADS_A * HEAD_DIM
WIDTH_KV_A = N_KV_A * HEAD_DIM
WIDTH_QI = N_IDX_HEADS * IDX_DIM
WIDTH_B = N_HEADS_B * HEAD_DIM
WIDTH_C = len(POOL_WINDOWS) * POOL_GROUP_DIM
IN_SPLITS = (WIDTH_A, WIDTH_KV_A, WIDTH_KV_A, WIDTH_QI, IDX_DIM, N_IDX_HEADS,
             WIDTH_B, WIDTH_B, WIDTH_B, WIDTH_C)

LANES = 128
SUBLANES = 8
VMEM_LIMIT_BYTES = 56 * 1024 * 1024

INT_MIN = -2 ** 31
MASKED_LOGIT = -1e30
F32_EXP_UNDERFLOW = -104.0

ROW_TILE = 512
ATT_TILE = 128
DENSE_TILE = 256
SAMPLE_PAD = 8
SAMPLE_UNIT = 64

_PERM_ORDER = (0, 1, 2, 3, 6, 7, 8, 9, 4, 5)
_PERM_WIDTH = 19 * LANES


def _params(n_axes):
    return pltpu.CompilerParams(
        dimension_semantics=("arbitrary",) * n_axes,
        vmem_limit_bytes=VMEM_LIMIT_BYTES)


def _const_spec(a, n_grid):
    zeros = (0,) * a.ndim
    if n_grid == 1:
        return pl.BlockSpec(a.shape, lambda i: zeros, pipeline_mode=pl.Buffered(1))
    return pl.BlockSpec(a.shape, lambda b, i: zeros, pipeline_mode=pl.Buffered(1))


def _rms(x, g):
    r = lax.rsqrt(jnp.mean(x * x, axis=-1, keepdims=True) + EPS)
    return (x * r) * g


def _sigmoid(x):
    return 1.0 / (1.0 + jnp.exp(-x))


def _dot(a, b):
    return jnp.dot(a, b, preferred_element_type=F32)


def _dot_t(a, b):
    return lax.dot_general(a, b, (((1,), (1,)), ((), ())), preferred_element_type=F32)


def _split_dot(x, m):
    hi = x.astype(BF16)
    lo = (x - hi.astype(F32)).astype(BF16)
    return _dot(hi, m) + _dot(lo, m)


def _in_proj_body(x_ref, g_ref, w_ref, cos_ref, sin_ref, gq_ref, gk_ref,
                  qa_ref, ka_ref, va_ref, qi_ref, qb_ref, kb_ref, vb_ref, u_ref, kiwi_ref):
    tm = x_ref.shape[0]
    a = _rms(x_ref[...], g_ref[...]).astype(BF16)
    proj = _dot(a, w_ref[...])
    cos = cos_ref[...]
    sin = sin_ref[...]
    lane = lax.broadcasted_iota(I32, (tm, LANES), 1)
    first_half = (lane % HEAD_DIM) < (HEAD_DIM // 2)
    r = lax.broadcasted_iota(I32, (LANES, LANES), 0) // HEAD_DIM
    c = lax.broadcasted_iota(I32, (LANES, LANES), 1) // HEAD_DIM
    same_head = (r == c).astype(BF16)

    def chunk(j):
        return proj[:, j * LANES:(j + 1) * LANES]

    def rope(x):
        ahead = pltpu.roll(x, LANES - HEAD_DIM // 2, 1)
        behind = pltpu.roll(x, HEAD_DIM // 2, 1)
        return x * cos + jnp.where(first_half, ahead, behind) * sin

    def head_norm(x, g):
        ms = _split_dot(x * x, same_head) * (1.0 / HEAD_DIM)
        return (x * lax.rsqrt(ms + EPS)) * g

    for j in range(3):
        qa_ref[:, j * LANES:(j + 1) * LANES] = rope(head_norm(chunk(j), gq_ref[...]))
    ka_ref[...] = rope(head_norm(chunk(3), gk_ref[...]))
    va_ref[...] = chunk(4)
    for j in range(2):
        qi_ref[:, j * LANES:(j + 1) * LANES] = rope(chunk(5 + j))
    for j in range(3):
        qb_ref[:, j * LANES:(j + 1) * LANES] = chunk(7 + j)
        kb_ref[:, j * LANES:(j + 1) * LANES] = chunk(10 + j)
        vb_ref[:, j * LANES:(j + 1) * LANES] = chunk(13 + j)
    for j in range(2):
        u_ref[:, j * LANES:(j + 1) * LANES] = chunk(16 + j)
    kw = chunk(18)
    kiwi_ref[...] = jnp.where(lane < IDX_DIM, rope(kw), kw * (N_IDX_HEADS ** -0.5))


def _in_proj(x, g, w_perm, cos_t, sin_t, gq, gk, tm):
    rows, d = x.shape
    n = rows // tm
    table_tiles = cos_t.shape[0] // tm
    widths = (WIDTH_A, WIDTH_KV_A, WIDTH_KV_A, WIDTH_QI, WIDTH_B, WIDTH_B, WIDTH_B, WIDTH_C, LANES)

    def row(w):
        return pl.BlockSpec((tm, w), lambda i: (i, 0))

    table = pl.BlockSpec((tm, LANES), lambda i: (i % table_tiles, 0))
    return pl.pallas_call(
        _in_proj_body,
        grid=(n,),
        in_specs=[row(d), _const_spec(g, 1), _const_spec(w_perm, 1), table, table,
                  _const_spec(gq, 1), _const_spec(gk, 1)],
        out_specs=[row(w) for w in widths],
        out_shape=[jax.ShapeDtypeStruct((rows, w), F32) for w in widths],
        compiler_params=_params(1),
        name="in_proj",
    )(x, g, w_perm, cos_t, sin_t, gq, gk)


def _half_select(x, src_half, dst_half, lane):
    if src_half != dst_half:
        x = pltpu.roll(x, HEAD_DIM, 1)
    return jnp.where((lane // HEAD_DIM) == dst_half, x, 0.0)


def _dsa_core(qi, wi, qa, q_pos, nkt, topk, ki_tile, k_tile, v_tile,
              key_ref, bias_ref, m_ref, l_ref, acc_ref, o_ref):
    tq = qi.shape[0]
    tk = key_ref.shape[2]
    group = N_HEADS_A // N_KV_A
    lane_k = lax.broadcasted_iota(I32, (tq, tk), 1)
    lane_q = lax.broadcasted_iota(I32, (tq, LANES), 1)

    qi_h = [qi[:, h * IDX_DIM:(h + 1) * IDX_DIM].astype(BF16) for h in range(N_IDX_HEADS)]
    wi_h = [wi[:, h:h + 1] for h in range(N_IDX_HEADS)]

    def score_tile(kt, carry):
        ki = ki_tile(kt).astype(BF16)
        s = jnp.zeros((tq, tk), F32)
        for h in range(N_IDX_HEADS):
            d = _dot_t(qi_h[h], ki) * (IDX_DIM ** -0.5)
            s = s + jnp.maximum(d, 0.0) * wi_h[h]
        bits = lax.bitcast_convert_type(s, I32)
        key = jnp.where(bits < 0, bits ^ 0x7FFFFFFF, bits)
        key = jnp.where(s == 0.0, 0, key)
        key_ref[kt] = jnp.where(kt * tk + lane_k <= q_pos, key, INT_MIN)
        return carry

    lax.fori_loop(0, nkt, score_tile, 0)

    def count(pred):
        def body(kt, acc):
            return acc + jnp.where(pred(key_ref[kt]), 1.0, 0.0)
        acc = lax.fori_loop(0, nkt, body, jnp.zeros((tq, tk), F32))
        return jnp.sum(acc, axis=1, keepdims=True)

    def bit_step(b, thr):
        cand = thr + lax.shift_left(jnp.int32(1), 31 - b)
        cand_b = jnp.broadcast_to(cand, (tq, tk))
        cnt = count(lambda k: k >= cand_b)
        return jnp.where(cnt >= topk, cand, thr)

    thr = lax.fori_loop(0, 32, bit_step, jnp.full((tq, 1), INT_MIN, I32))
    thr_b = jnp.broadcast_to(thr, (tq, tk))

    n_ties = topk - count(lambda k: k > thr_b)
    r = lax.broadcasted_iota(I32, (tk, tk), 0)
    c = lax.broadcasted_iota(I32, (tk, tk), 1)
    prefix_incl = (r <= c).astype(BF16)

    def mask_tile(kt, seen):
        k = key_ref[kt]
        eq = k == thr_b
        eq_f = jnp.where(eq, 1.0, 0.0)
        rank = _dot(eq_f.astype(BF16), prefix_incl) + seen
        sel = ((k > thr_b) | (eq & (rank <= n_ties))) & (k != INT_MIN)
        bias_ref[kt] = jnp.where(sel, 0.0, MASKED_LOGIT)
        return seen + jnp.sum(eq_f, axis=1, keepdims=True)

    lax.fori_loop(0, nkt, mask_tile, jnp.zeros((tq, 1), F32))

    heads_out = [None] * N_HEADS_A
    for n in range(N_KV_A):
        q_rows = []
        for h in range(n * group, (n + 1) * group):
            chunk = qa[:, (h // 2) * LANES:(h // 2 + 1) * LANES]
            q_rows.append(_half_select(chunk, h % 2, n, lane_q))
        q_stack = jnp.concatenate(q_rows, axis=0).astype(BF16)
        m_ref[...] = jnp.full(m_ref.shape, MASKED_LOGIT, F32)
        l_ref[...] = jnp.zeros(l_ref.shape, F32)
        acc_ref[...] = jnp.zeros(acc_ref.shape, F32)

        def attend(kt, carry):
            k = k_tile(kt).astype(BF16)
            v = v_tile(kt).astype(BF16)
            b = bias_ref[kt]
            lg = _dot_t(q_stack, k) * (HEAD_DIM ** -0.5) + jnp.concatenate([b] * group, axis=0)
            m_old = m_ref[...]
            m_new = jnp.maximum(m_old, jnp.max(lg, axis=1, keepdims=True))
            alpha = jnp.exp(m_old - m_new)
            p = jnp.exp(lg - m_new)
            l_ref[...] = alpha * l_ref[...] + jnp.sum(p, axis=1, keepdims=True)
            acc_ref[...] = alpha * acc_ref[...] + _dot(p.astype(BF16), v)
            m_ref[...] = m_new
            return carry

        lax.fori_loop(0, nkt, attend, 0)
        o = acc_ref[...] / l_ref[...]
        for j in range(group):
            heads_out[n * group + j] = (n, o[j * tq:(j + 1) * tq])

    for cc in range(N_HEADS_A // 2):
        parts = []
        for h in (2 * cc, 2 * cc + 1):
            src_half, x = heads_out[h]
            parts.append(_half_select(x, src_half, h % 2, lane_q))
        o_ref[:, cc * LANES:(cc + 1) * LANES] = parts[0] + parts[1]


def _dsa_scratch(tq, tk, nkt_max):
    group = N_HEADS_A // N_KV_A
    return [pltpu.VMEM((nkt_max, tq, tk), I32),
            pltpu.VMEM((nkt_max, tq, tk), F32),
            pltpu.VMEM((group * tq, 1), F32),
            pltpu.VMEM((group * tq, 1), F32),
            pltpu.VMEM((group * tq, LANES), F32)]


def _dsa_prompt_body(qi_ref, kiwq_ref, qa_ref, kiwi_ref, ka_ref, va_ref, o_ref,
                     key_ref, bias_ref, m_ref, l_ref, acc_ref, *, topk):
    tq = qi_ref.shape[0]
    i = pl.program_id(1)
    q_pos = i * tq + lax.broadcasted_iota(I32, (tq, 1), 0)

    def rows(kt):
        return pl.ds(pl.multiple_of(kt * tq, tq), tq)

    _dsa_core(qi_ref[...], kiwq_ref[:, IDX_DIM:IDX_DIM + N_IDX_HEADS], qa_ref[...], q_pos, i + 1, topk,
              lambda kt: kiwi_ref[rows(kt), 0:IDX_DIM],
              lambda kt: ka_ref[rows(kt), :],
              lambda kt: va_ref[rows(kt), :],
              key_ref, bias_ref, m_ref, l_ref, acc_ref, o_ref)


def _dsa_prompt(qi, kiwi, qa, ka, va, batch, seq):
    tq = ATT_TILE
    nq = seq // tq

    def q_spec(w):
        return pl.BlockSpec((tq, w), lambda b, i: (b * nq + i, 0))

    def k_spec(w):
        return pl.BlockSpec((seq, w), lambda b, i: (b, 0))

    return pl.pallas_call(
        functools.partial(_dsa_prompt_body, topk=float(min(TOPK_MAX, seq // 4))),
        grid=(batch, nq),
        in_specs=[q_spec(WIDTH_QI), q_spec(LANES), q_spec(WIDTH_A),
                  k_spec(LANES), k_spec(WIDTH_KV_A), k_spec(WIDTH_KV_A)],
        out_specs=q_spec(WIDTH_A),
        out_shape=jax.ShapeDtypeStruct((batch * seq, WIDTH_A), F32),
        scratch_shapes=_dsa_scratch(tq, tq, nq),
        compiler_params=_params(2),
        name="dsa_prompt",
    )(qi, kiwi, qa, kiwi, ka, va)


def _gather_pages(pt_ref, b, layer, caches, bufs, sem, n_pages, page):
    copies = []
    for ci, (cache, buf) in enumerate(zip(caches, bufs)):
        for p in range(n_pages):
            copies.append(pltpu.make_async_copy(
                cache.at[layer, pt_ref[b, p]], buf.at[pl.ds(p * page, page)], sem.at[ci, p]))
    for cp in copies:
        cp.start()
    for cp in copies:
        cp.wait()


def _dsa_sample_body(pt_ref, qi_ref, kiwi_ref, qa_ref, ka_ref, va_ref, cidx_ref, cak_ref, cav_ref,
                     o_ref, ki_buf, ka_buf, va_buf, sem, key_ref, bias_ref, m_ref, l_ref, acc_ref,
                     *, layer, n_pages, page, topk):
    b = pl.program_id(0)
    past = n_pages * page
    tq = qi_ref.shape[0]
    ki_buf[pl.ds(past, page), :] = jnp.zeros((page, IDX_DIM), F32)
    ka_buf[pl.ds(past, page), :] = jnp.zeros((page, WIDTH_KV_A), F32)
    va_buf[pl.ds(past, page), :] = jnp.zeros((page, WIDTH_KV_A), F32)
    ki_buf[pl.ds(past, tq), :] = kiwi_ref[:, 0:IDX_DIM]
    ka_buf[pl.ds(past, tq), :] = ka_ref[...]
    va_buf[pl.ds(past, tq), :] = va_ref[...]
    _gather_pages(pt_ref, b, layer, (cidx_ref, cak_ref, cav_ref), (ki_buf, ka_buf, va_buf),
                  sem, n_pages, page)
    q_pos = past + lax.broadcasted_iota(I32, (tq, 1), 0)

    def rows(kt):
        return pl.ds(pl.multiple_of(kt * page, page), page)

    _dsa_core(qi_ref[...], kiwi_ref[:, IDX_DIM:IDX_DIM + N_IDX_HEADS], qa_ref[...], q_pos, n_pages + 1, topk,
              lambda kt: ki_buf[rows(kt), :],
              lambda kt: ka_buf[rows(kt), :],
              lambda kt: va_buf[rows(kt), :],
              key_ref, bias_ref, m_ref, l_ref, acc_ref, o_ref)


def _dsa_sample(page_table, qi, kiwi, qa, ka, va, cache_idx, cache_ak, cache_av, layer, n_new):
    nb, n_pages = page_table.shape
    page = cache_idx.shape[2]
    tq = SAMPLE_PAD
    total = (n_pages + 1) * page

    def q_spec(w):
        return pl.BlockSpec((None, tq, w), lambda b, pt: (b, 0, 0))

    hbm = pl.BlockSpec(memory_space=pl.ANY)
    grid_spec = pltpu.PrefetchScalarGridSpec(
        num_scalar_prefetch=1,
        grid=(nb,),
        in_specs=[q_spec(WIDTH_QI), q_spec(LANES), q_spec(WIDTH_A), q_spec(WIDTH_KV_A), q_spec(WIDTH_KV_A),
                  hbm, hbm, hbm],
        out_specs=q_spec(WIDTH_A),
        scratch_shapes=[pltpu.VMEM((total, IDX_DIM), F32),
                        pltpu.VMEM((total, WIDTH_KV_A), F32),
                        pltpu.VMEM((total, WIDTH_KV_A), F32),
                        pltpu.SemaphoreType.DMA((3, n_pages))] + _dsa_scratch(tq, page, n_pages + 1),
    )
    topk = float(min(TOPK_MAX, (n_pages * page + n_new) // 4))
    return pl.pallas_call(
        functools.partial(_dsa_sample_body, layer=layer, n_pages=n_pages, page=page, topk=topk),
        grid_spec=grid_spec,
        out_shape=jax.ShapeDtypeStruct((nb, tq, WIDTH_A), F32),
        compiler_params=_params(1),
        name="dsa_sample",
    )(page_table, qi, kiwi, qa, ka, va, cache_idx, cache_ak, cache_av)


def _sb_core(qb, q_pos, kt_last, k_tile, v_tile, carry_ref, acc_ref, o_ref, tk):
    tq = qb.shape[0]
    lane_q = lax.broadcasted_iota(I32, (tq, LANES), 1)
    lane_k = lax.broadcasted_iota(I32, (tq, tk), 1)
    r = lax.broadcasted_iota(I32, (tk, tk), 0)
    c = lax.broadcasted_iota(I32, (tk, tk), 1)
    suffix_incl = (r >= c).astype(BF16)
    q_pad = []
    for h in range(N_HEADS_B):
        chunk = qb[:, (h // 2) * LANES:(h // 2 + 1) * LANES]
        q_pad.append(jnp.where((lane_q // HEAD_DIM) == (h % 2), chunk, 0.0).astype(BF16))
    carry_ref[...] = jnp.zeros(carry_ref.shape, F32)
    acc_ref[...] = jnp.zeros(acc_ref.shape, F32)

    def cond(state):
        kt, worst = state
        return jnp.logical_and(kt >= 0, worst > F32_EXP_UNDERFLOW)

    def body(state):
        kt, _ = state
        strict = kt * tk + lane_k < q_pos
        worst = jnp.full((tq, 1), -jnp.inf, F32)
        for cc in range(N_HEADS_B // 2):
            k = k_tile(kt, cc).astype(BF16)
            v = v_tile(kt, cc).astype(BF16)
            for h in (2 * cc, 2 * cc + 1):
                z = _dot_t(q_pad[h], k) * (HEAD_DIM ** -0.5)
                softplus = jnp.maximum(z, 0.0) + jnp.log1p(jnp.exp(-jnp.abs(z)))
                log_rem = jnp.where(strict, -softplus, 0.0)
                rev = _split_dot(log_rem, suffix_incl)
                carry = carry_ref[h]
                a = jnp.where(strict, jnp.exp(z + rev + carry), 0.0)
                acc_ref[h] = acc_ref[h] + _dot(a.astype(BF16), v)
                carry = carry + rev[:, 0:1]
                carry_ref[h] = carry
                worst = jnp.maximum(worst, carry)
        return kt - 1, jnp.max(worst)

    lax.while_loop(cond, body, (kt_last, jnp.float32(0.0)))
    for cc in range(N_HEADS_B // 2):
        o_ref[:, cc * LANES:(cc + 1) * LANES] = jnp.where(
            lane_q < HEAD_DIM, acc_ref[2 * cc], acc_ref[2 * cc + 1])


def _sb_scratch(tq):
    return [pltpu.VMEM((N_HEADS_B, tq, 1), F32), pltpu.VMEM((N_HEADS_B, tq, LANES), F32)]


def _sb_prompt_body(qb_ref, kb_ref, vb_ref, o_ref, carry_ref, acc_ref):
    tq = qb_ref.shape[0]
    i = pl.program_id(1)
    q_pos = i * tq + lax.broadcasted_iota(I32, (tq, 1), 0)

    def tile(ref):
        return lambda kt, cc: ref[pl.ds(pl.multiple_of(kt * tq, tq), tq), cc * LANES:(cc + 1) * LANES]

    _sb_core(qb_ref[...], q_pos, i, tile(kb_ref), tile(vb_ref), carry_ref, acc_ref, o_ref, tq)


def _sb_prompt(qb, kb, vb, batch, seq):
    tq = ATT_TILE
    nq = seq // tq
    q_spec = pl.BlockSpec((tq, WIDTH_B), lambda b, i: (b * nq + i, 0))
    k_spec = pl.BlockSpec((seq, WIDTH_B), lambda b, i: (b, 0))
    return pl.pallas_call(
        _sb_prompt_body,
        grid=(batch, nq),
        in_specs=[q_spec, k_spec, k_spec],
        out_specs=q_spec,
        out_shape=jax.ShapeDtypeStruct((batch * seq, WIDTH_B), F32),
        scratch_shapes=_sb_scratch(tq),
        compiler_params=_params(2),
        name="sb_prompt",
    )(qb, kb, vb)


def _sb_sample_body(pt_ref, qb_ref, kb_ref, vb_ref, ck_ref, cv_ref, o_ref,
                    k_buf, v_buf, sem, carry_ref, acc_ref, *, layer, n_pages, page):
    b = pl.program_id(0)
    past = n_pages * page
    tq = qb_ref.shape[0]
    k_buf[pl.ds(past, page), :] = jnp.zeros((page, WIDTH_B), F32)
    v_buf[pl.ds(past, page), :] = jnp.zeros((page, WIDTH_B), F32)
    k_buf[pl.ds(past, tq), :] = kb_ref[...]
    v_buf[pl.ds(past, tq), :] = vb_ref[...]
    _gather_pages(pt_ref, b, layer, (ck_ref, cv_ref), (k_buf, v_buf), sem, n_pages, page)
    q_pos = past + lax.broadcasted_iota(I32, (tq, 1), 0)

    def tile(ref):
        return lambda kt, cc: ref[pl.ds(pl.multiple_of(kt * page, page), page), cc * LANES:(cc + 1) * LANES]

    _sb_core(qb_ref[...], q_pos, n_pages, tile(k_buf), tile(v_buf), carry_ref, acc_ref, o_ref, page)


def _sb_sample(page_table, qb, kb, vb, cache_k, cache_v, layer):
    nb, n_pages = page_table.shape
    page = cache_k.shape[2]
    tq = SAMPLE_PAD
    total = (n_pages + 1) * page
    q_spec = pl.BlockSpec((None, tq, WIDTH_B), lambda b, pt: (b, 0, 0))
    hbm = pl.BlockSpec(memory_space=pl.ANY)
    grid_spec = pltpu.PrefetchScalarGridSpec(
        num_scalar_prefetch=1,
        grid=(nb,),
        in_specs=[q_spec, q_spec, q_spec, hbm, hbm],
        out_specs=q_spec,
        scratch_shapes=[pltpu.VMEM((total, WIDTH_B), F32),
                        pltpu.VMEM((total, WIDTH_B), F32),
                        pltpu.SemaphoreType.DMA((2, n_pages))] + _sb_scratch(tq),
    )
    return pl.pallas_call(
        functools.partial(_sb_sample_body, layer=layer, n_pages=n_pages, page=page),
        grid_spec=grid_spec,
        out_shape=jax.ShapeDtypeStruct((nb, tq, WIDTH_B), F32),
        compiler_params=_params(1),
        name="sb_sample",
    )(page_table, qb, kb, vb, cache_k, cache_v)


def _mix_body(h_ref, oa_ref, ob_ref, u_ref, halo_ref, woa_ref, wob_ref, woc_ref, wpool_ref, pscale_ref,
              o_ref, ext_a, ext_b, *, unit, tiles_per_seq, pos0, zero_first_halo):
    tm = h_ref.shape[0]
    i = pl.program_id(0)
    tile_in_seq = i % tiles_per_seq
    lo = 8 * unit
    base = 24 * unit
    n = base + tm
    u = u_ref[...]
    halo = halo_ref[...]
    if zero_first_halo:
        halo = jnp.where(tile_in_seq == 0, 0.0, halo)
    zeros = jnp.zeros((lo, WIDTH_C), F32)
    ext_a[pl.ds(0, lo), :] = zeros
    ext_b[pl.ds(0, lo), :] = zeros
    ext_a[pl.ds(lo, 16 * unit), :] = halo
    ext_a[pl.ds(base, tm), :] = u

    def doubled(src, shift):
        return src[pl.ds(lo, n - lo), :] + src[pl.ds(lo - shift * unit, n - lo), :]

    cur = 16 * unit
    s2 = doubled(ext_a, 1)
    ext_b[pl.ds(lo, n - lo), :] = s2
    s4 = doubled(ext_b, 2)
    ext_a[pl.ds(lo, n - lo), :] = s4
    s8 = doubled(ext_a, 4)
    ext_b[pl.ds(lo, n - lo), :] = s8
    s16 = doubled(ext_b, 8)
    lane = lax.broadcasted_iota(I32, (tm, WIDTH_C), 1)
    group = lane // POOL_GROUP_DIM
    sums = jnp.where(group == 0, s2[cur:], jnp.where(group == 1, s4[cur:],
                     jnp.where(group == 2, s8[cur:], s16[cur:])))
    window = jnp.where(group == 0, POOL_WINDOWS[0], jnp.where(group == 1, POOL_WINDOWS[1],
                       jnp.where(group == 2, POOL_WINDOWS[2], POOL_WINDOWS[3])))
    token = tile_in_seq * (tm // unit) + lax.broadcasted_iota(I32, (tm, WIDTH_C), 0) // unit
    cnt = jnp.minimum(window, pos0 + token + 1).astype(F32)
    pooled = (sums / cnt - u).astype(BF16)
    y = (_dot(pooled, wpool_ref[...]) * pscale_ref[...]).astype(BF16)
    mix = (_dot(oa_ref[...].astype(BF16), woa_ref[...]) + _dot(ob_ref[...].astype(BF16), wob_ref[...])
           + _dot(y, woc_ref[...]))
    o_ref[...] = h_ref[...] + mix


def _mix(h, oa, ob, u, halo, woa, wob, woc, wpool, pscale, *, unit, tiles_per_seq, pos0, prompt):
    rows, d = h.shape
    tm = DENSE_TILE
    n = rows // tm

    def row(w):
        return pl.BlockSpec((tm, w), lambda i: (i, 0))

    if prompt:
        per = tm // (16 * unit)
        halo_spec = pl.BlockSpec((16 * unit, WIDTH_C), lambda i: (jnp.maximum(i * per - 1, 0), 0))
    else:
        halo_spec = pl.BlockSpec((16 * unit, WIDTH_C), lambda i: (i, 0))
    consts = (woa, wob, woc, wpool, pscale)
    return pl.pallas_call(
        functools.partial(_mix_body, unit=unit, tiles_per_seq=tiles_per_seq, pos0=pos0, zero_first_halo=prompt),
        grid=(n,),
        in_specs=[row(d), row(WIDTH_A), row(WIDTH_B), row(WIDTH_C), halo_spec] + [_const_spec(a, 1) for a in consts],
        out_specs=row(d),
        out_shape=jax.ShapeDtypeStruct((rows, d), F32),
        scratch_shapes=[pltpu.VMEM((24 * unit + tm, WIDTH_C), F32), pltpu.VMEM((24 * unit + tm, WIDTH_C), F32)],
        compiler_params=_params(1),
        name="mix",
    )(h, oa, ob, u, halo, *consts)


def _ffn_body(h_ref, p_ref, pre_ref, gffn_ref, wup_ref, cw_ref, cb_ref, wdn_ref, gple_ref, wpg_ref, wpp_ref,
              o_ref, tail_ref, ext_ref, *, unit, tiles_per_seq):
    tm = h_ref.shape[0]
    halo = pre_ref.shape[0]
    dff = cw_ref.shape[1]
    i = pl.program_id(0)
    h = h_ref[...]
    up = _dot(_rms(h, gffn_ref[...]).astype(BF16), wup_ref[...])
    gate = up[:, :dff]
    val = up[:, dff:]

    @pl.when(i % tiles_per_seq == 0)
    def _():
        ext_ref[pl.ds(0, halo), :] = pre_ref[...]

    ext_ref[pl.ds(halo, tm), :] = gate
    conv = cb_ref[...] + ext_ref[pl.ds(halo - 2 * unit, tm), :] * cw_ref[0:1, :]
    conv = conv + ext_ref[pl.ds(halo - unit, tm), :] * cw_ref[1:2, :]
    conv = conv + gate * cw_ref[2:3, :]
    act = (conv * _sigmoid(conv) * val).astype(BF16)
    h2 = h + _dot(act, wdn_ref[...])
    tail = ext_ref[pl.ds(tm, halo), :]
    tail_ref[...] = tail
    ext_ref[pl.ds(0, halo), :] = tail
    gt = _sigmoid(_dot(_rms(h2, gple_ref[...]).astype(BF16), wpg_ref[...]))
    o_ref[...] = h2 + gt * _dot(p_ref[...].astype(BF16), wpp_ref[...])


def _ffn(h, p, pre, gffn, wup, cw, cb, wdn, gple, wpg, wpp, *, unit, tiles_per_seq, pre_per_seq):
    rows, d = h.shape
    tm = DENSE_TILE
    n = rows // tm
    halo = max(SUBLANES, (CONV_WIDTH - 1) * unit)
    dff = cw.shape[1]

    def row(w):
        return pl.BlockSpec((tm, w), lambda i: (i, 0))

    if pre_per_seq:
        pre_spec = pl.BlockSpec((halo, dff), lambda i: (i // tiles_per_seq, 0))
    else:
        pre_spec = pl.BlockSpec((halo, dff), lambda i: (0, 0))
    tail_spec = pl.BlockSpec((halo, dff), lambda i: (i // tiles_per_seq, 0))
    consts = (gffn, wup, cw, cb, wdn, gple, wpg, wpp)
    return pl.pallas_call(
        functools.partial(_ffn_body, unit=unit, tiles_per_seq=tiles_per_seq),
        grid=(n,),
        in_specs=[row(d), row(p.shape[1]), pre_spec] + [_const_spec(a, 1) for a in consts],
        out_specs=[row(d), tail_spec],
        out_shape=[jax.ShapeDtypeStruct((rows, d), F32),
                   jax.ShapeDtypeStruct((n // tiles_per_seq * halo, dff), F32)],
        scratch_shapes=[pltpu.VMEM((halo + tm, dff), F32)],
        compiler_params=_params(1),
        name="ffn",
    )(h, p, pre, *consts)


def _rope_tables(pos):
    half = HEAD_DIM // 2
    inv = ROPE_THETA ** (-jnp.arange(half, dtype=F32) / half)
    ang = pos.astype(F32)[:, None] * inv[None, :]
    cos = jnp.cos(ang)
    sin = jnp.sin(ang)
    cos_t = jnp.tile(cos, (1, LANES // half))
    sin_t = jnp.tile(jnp.concatenate([-sin, sin], axis=1), (1, LANES // HEAD_DIM))
    return cos_t, sin_t


def _layer_weights(l, g_attn, w_in, g_q_a, g_k_a, w_pool, pool_scale, w_out,
                   g_ffn, w_up, conv_w, conv_b, w_down, g_ple, w_ple_gate, w_ple_proj):
    starts = [0]
    for w in IN_SPLITS:
        starts.append(starts[-1] + w)
    cols = [w_in[l][:, starts[j]:starts[j + 1]] for j in _PERM_ORDER]
    w_perm = jnp.concatenate(cols, axis=1)
    w_perm = jnp.pad(w_perm, ((0, 0), (0, _PERM_WIDTH - w_perm.shape[1]))).astype(BF16)
    wpool = jax.scipy.linalg.block_diag(*[w_pool[l][g] for g in range(len(POOL_WINDOWS))]).astype(BF16)
    wo = w_out[l].astype(BF16)
    return dict(
        g_attn=g_attn[l][None, :], w_perm=w_perm,
        gq=jnp.tile(g_q_a[l], LANES // HEAD_DIM)[None, :], gk=jnp.tile(g_k_a[l], LANES // HEAD_DIM)[None, :],
        wpool=wpool, pscale=pool_scale[l][None, :],
        woa=wo[:WIDTH_A], wob=wo[WIDTH_A:WIDTH_A + WIDTH_B], woc=wo[WIDTH_A + WIDTH_B:],
        gffn=g_ffn[l][None, :], wup=w_up[l].astype(BF16), cw=conv_w[l], cb=conv_b[l][None, :],
        wdn=w_down[l].astype(BF16), gple=g_ple[l][None, :],
        wpg=w_ple_gate[l].astype(BF16), wpp=w_ple_proj[l].astype(BF16))


def _to_token_major(x, n_tok):
    nb, _, w = x.shape
    x = x[:, :n_tok].reshape(nb // SAMPLE_UNIT, SAMPLE_UNIT, n_tok, w)
    return x.transpose(0, 2, 1, 3).reshape(nb * n_tok, w)


def _state_token_major(x):
    nb, n_tok, w = x.shape
    return x.reshape(nb // SAMPLE_UNIT, SAMPLE_UNIT, n_tok, w).transpose(0, 2, 1, 3)


def _from_token_major(x, nb, n_tok):
    w = x.shape[1]
    x = x.reshape(nb // SAMPLE_UNIT, n_tok, SAMPLE_UNIT, w).transpose(0, 2, 1, 3)
    return x.reshape(nb, n_tok, w)


def kernel(x_prompt, x_sample, cache_a_k, cache_a_v, cache_idx_k, cache_b_k, cache_b_v, state_pool, state_conv, page_table, p_prompt, p_sample, g_attn, w_in, g_q_a, g_k_a, w_pool, pool_scale, w_out, g_ffn, w_up, conv_w, conv_b, w_down, g_ple, w_ple_gate, w_ple_proj):
    batch, seq, d = x_prompt.shape
    nb, n_new, _ = x_sample.shape
    depth, n_pool, page = cache_idx_k.shape[:3]
    n_pages = page_table.shape[1]
    past = n_pages * page
    dff = conv_w.shape[2]
    assert seq % ROW_TILE == 0 and seq % DENSE_TILE == 0 and seq % ATT_TILE == 0
    assert nb % SAMPLE_UNIT == 0 and n_new <= SAMPLE_PAD and page == ATT_TILE
    assert (n_new * SAMPLE_UNIT) == DENSE_TILE and (nb * SAMPLE_PAD) % ROW_TILE == 0

    cos_p, sin_p = _rope_tables(jnp.arange(seq, dtype=I32))
    cos_s, sin_s = _rope_tables(past + jnp.arange(nb * SAMPLE_PAD, dtype=I32) % SAMPLE_PAD)
    cak = cache_a_k.reshape(depth, n_pool, page, WIDTH_KV_A)
    cav = cache_a_v.reshape(depth, n_pool, page, WIDTH_KV_A)
    cbk = cache_b_k.reshape(depth, n_pool, page, WIDTH_B)
    cbv = cache_b_v.reshape(depth, n_pool, page, WIDTH_B)

    h_p = x_prompt.reshape(batch * seq, d)
    h_s = jnp.pad(x_sample, ((0, 0), (0, SAMPLE_PAD - n_new), (0, 0))).reshape(nb * SAMPLE_PAD, d)
    zero_pre = jnp.zeros((SUBLANES, dff), F32)
    rows_p, rows_s = [], []
    for l in range(depth):
        w = _layer_weights(l, g_attn, w_in, g_q_a, g_k_a, w_pool, pool_scale, w_out,
                           g_ffn, w_up, conv_w, conv_b, w_down, g_ple, w_ple_gate, w_ple_proj)
        ffn_w = (w["gffn"], w["wup"], w["cw"], w["cb"], w["wdn"], w["gple"], w["wpg"], w["wpp"])
        mix_w = (w["woa"], w["wob"], w["woc"], w["wpool"], w["pscale"])

        qa, ka, va, qi, qb, kb, vb, u, kiwi = _in_proj(
            h_p, w["g_attn"], w["w_perm"], cos_p, sin_p, w["gq"], w["gk"], ROW_TILE)
        o_a = _dsa_prompt(qi, kiwi, qa, ka, va, batch, seq)
        o_b = _sb_prompt(qb, kb, vb, batch, seq)
        h1 = _mix(h_p, o_a, o_b, u, u, *mix_w, unit=1, tiles_per_seq=seq // DENSE_TILE, pos0=0, prompt=True)
        h_p, tail = _ffn(h1, p_prompt[l].reshape(batch * seq, -1), zero_pre, *ffn_w,
                         unit=1, tiles_per_seq=seq // DENSE_TILE, pre_per_seq=False)
        rows_p.append((
            ka.reshape(batch, seq, N_KV_A, HEAD_DIM), va.reshape(batch, seq, N_KV_A, HEAD_DIM),
            kiwi[:, :IDX_DIM].reshape(batch, seq, IDX_DIM),
            kb.reshape(batch, seq, N_HEADS_B, HEAD_DIM), vb.reshape(batch, seq, N_HEADS_B, HEAD_DIM),
            u.reshape(batch, seq, WIDTH_C)[:, seq - POOL_STATE:],
            tail.reshape(batch, SUBLANES, dff)[:, SUBLANES - (CONV_WIDTH - 1):]))

        qa, ka, va, qi, qb, kb, vb, u, kiwi = _in_proj(
            h_s, w["g_attn"], w["w_perm"], cos_s, sin_s, w["gq"], w["gk"], ROW_TILE)

        def nat(x):
            return x.reshape(nb, SAMPLE_PAD, x.shape[1])

        o_a = _dsa_sample(page_table, nat(qi), nat(kiwi), nat(qa), nat(ka), nat(va),
                          cache_idx_k, cak, cav, l, n_new)
        o_b = _sb_sample(page_table, nat(qb), nat(kb), nat(vb), cbk, cbv, l)
        u_nat = nat(u)[:, :n_new]
        pool_state = _state_token_major(state_pool[l])
        halo = jnp.pad(pool_state, ((0, 0), (1, 0), (0, 0), (0, 0))).reshape(-1, WIDTH_C)
        h1 = _mix(_to_token_major(nat(h_s), n_new), _to_token_major(o_a, n_new), _to_token_major(o_b, n_new),
                  _to_token_major(nat(u), n_new), halo, *mix_w,
                  unit=SAMPLE_UNIT, tiles_per_seq=1, pos0=past, prompt=False)
        pre = _state_token_major(state_conv[l]).reshape(-1, dff)
        h3, tail = _ffn(h1, _to_token_major(p_sample[l], n_new), pre, *ffn_w,
                        unit=SAMPLE_UNIT, tiles_per_seq=1, pre_per_seq=True)
        h_s_nat = _from_token_major(h3, nb, n_new)
        h_s = jnp.pad(h_s_nat, ((0, 0), (0, SAMPLE_PAD - n_new), (0, 0))).reshape(nb * SAMPLE_PAD, d)
        conv_state = tail.reshape(nb // SAMPLE_UNIT, CONV_WIDTH - 1, SAMPLE_UNIT, dff)
        conv_state = conv_state.transpose(0, 2, 1, 3).reshape(nb, CONV_WIDTH - 1, dff)
        rows_s.append((
            nat(ka)[:, :n_new].reshape(nb, n_new, N_KV_A, HEAD_DIM),
            nat(va)[:, :n_new].reshape(nb, n_new, N_KV_A, HEAD_DIM),
            nat(kiwi)[:, :n_new, :IDX_DIM],
            nat(kb)[:, :n_new].reshape(nb, n_new, N_HEADS_B, HEAD_DIM),
            nat(vb)[:, :n_new].reshape(nb, n_new, N_HEADS_B, HEAD_DIM),
            jnp.concatenate([state_pool[l], u_nat], axis=1)[:, n_new:],
            conv_state))

    def stack(rows, j):
        return jnp.stack([r[j] for r in rows], axis=0)

    y_p = h_p.reshape(batch, seq, d)
    y_s = h_s.reshape(nb, SAMPLE_PAD, d)[:, :n_new]
    return (y_p, y_s) + tuple(stack(rows_p, j) for j in range(7)) + tuple(stack(rows_s, j) for j in range(7))
```

```python
import functools

import jax
import jax.numpy as jnp
from jax import lax
from jax.experimental import pallas as pl
from jax.experimental.pallas import tpu as pltpu

F32 = jnp.float32
BF16 = jnp.bfloat16
I32 = jnp.int32

HEAD_DIM = 64
N_HEADS_A = 6
N_KV_A = 2
N_HEADS_B = 6
N_IDX_HEADS = 4
IDX_DIM = 64
TOPK_MAX = 256
POOL_WINDOWS = (2, 4, 8, 16)
POOL_GROUP_DIM = 64
POOL_STATE = 15
CONV_WIDTH = 3
ROPE_THETA = 10000.0
EPS = 1e-6
WIDTH_A = N_HEADS_A * HEAD_DIM
WIDTH_KV_A = N_KV_A * HEAD_DIM
WIDTH_QI = N_IDX_HEADS * IDX_DIM
WIDTH_B = N_HEADS_B * HEAD_DIM
WIDTH_C = len(POOL_WINDOWS) * POOL_GROUP_DIM
IN_SPLITS = (WIDTH_A, WIDTH_KV_A, WIDTH_KV_A, WIDTH_QI, IDX_DIM, N_IDX_HEADS,
             WIDTH_B, WIDTH_B, WIDTH_B, WIDTH_C)
KV_GROUP = N_HEADS_A // N_KV_A

LANES = 128
SUBLANES = 8
VMEM_LIMIT_BYTES = 56 * 1024 * 1024

INT_MIN = -2 ** 31
MASKED_LOGIT = -1e30
F32_EXP_UNDERFLOW = -104.0

ROW_TILE = 512
ATT_TILE = 128
KEY_TILE = 256
DENSE_TILE = 256
CAST_TILE = 256
SAMPLE_PAD = 8
SAMPLE_UNIT = 64
SAMPLE_GROUP = 4
SB_EAGER_PAGES = 2

_PERM_ORDER = (0, 1, 2, 3, 6, 7, 8, 9, 4, 5)
_PERM_WIDTH = 19 * LANES


def _params(n_axes):
    return pltpu.CompilerParams(
        dimension_semantics=("arbitrary",) * n_axes,
        vmem_limit_bytes=VMEM_LIMIT_BYTES)


def _const_spec(a, n_grid):
    zeros = (0,) * a.ndim
    if n_grid == 1:
        return pl.BlockSpec(a.shape, lambda i: zeros, pipeline_mode=pl.Buffered(1))
    return pl.BlockSpec(a.shape, lambda b, i: zeros, pipeline_mode=pl.Buffered(1))


def _rms(x, g):
    r = lax.rsqrt(jnp.mean(x * x, axis=-1, keepdims=True) + EPS)
    return (x * r) * g


def _sigmoid(x):
    return 1.0 / (1.0 + jnp.exp(-x))


def _dot(a, b):
    return jnp.dot(a, b, preferred_element_type=F32)


def _dot_t(a, b):
    return lax.dot_general(a, b, (((1,), (1,)), ((), ())), preferred_element_type=F32)


def _split_dot(x, m):
    hi = x.astype(BF16)
    lo = (x - hi.astype(F32)).astype(BF16)
    return _dot(hi, m) + _dot(lo, m)


def _fold_rows(x, op):
    parts = [x[j * SUBLANES:(j + 1) * SUBLANES] for j in range(x.shape[0] // SUBLANES)]
    while len(parts) > 1:
        nxt = [op(parts[j], parts[j + 1]) for j in range(0, len(parts) - 1, 2)]
        if len(parts) % 2:
            nxt.append(parts[-1])
        parts = nxt
    return parts[0]


def _sort_key(s):
    bits = lax.bitcast_convert_type(s, I32)
    key = jnp.where(bits < 0, bits ^ 0x7FFFFFFF, bits)
    return jnp.where(s == 0.0, 0, key)


def _pad_rows(x, rows):
    return jnp.concatenate([x, jnp.zeros((rows - x.shape[0], x.shape[1]), x.dtype)], axis=0)


def _cast_body(x_ref, o_ref):
    o_ref[...] = x_ref[...].astype(BF16)


def _to_bf16(w):
    depth, rows, cols = w.shape
    tr = CAST_TILE if rows % CAST_TILE == 0 else rows
    spec = pl.BlockSpec((None, tr, cols), lambda l, i: (l, i, 0))
    return pl.pallas_call(
        _cast_body,
        grid=(depth, rows // tr),
        in_specs=[spec],
        out_specs=spec,
        out_shape=jax.ShapeDtypeStruct(w.shape, BF16),
        compiler_params=_params(2),
        name="cast_bf16",
    )(w)


def _in_proj_body(x_ref, g_ref, w_ref, cos_ref, sin_ref, gq_ref, gk_ref,
                  qa_ref, ka_ref, va_ref, qi_ref, qb_ref, kb_ref, vb_ref, u_ref, kiwi_ref):
    tm = x_ref.shape[0]
    a = _rms(x_ref[...], g_ref[...]).astype(BF16)
    proj = _dot(a, w_ref[...])
    cos = cos_ref[...]
    sin = sin_ref[...]
    lane = lax.broadcasted_iota(I32, (tm, LANES), 1)
    first_half = (lane % HEAD_DIM) < (HEAD_DIM // 2)
    r = lax.broadcasted_iota(I32, (LANES, LANES), 0) // HEAD_DIM
    c = lax.broadcasted_iota(I32, (LANES, LANES), 1) // HEAD_DIM
    same_head = (r == c).astype(BF16)

    def chunk(j):
        return proj[:, j * LANES:(j + 1) * LANES]

    def rope(x):
        ahead = pltpu.roll(x, LANES - HEAD_DIM // 2, 1)
        behind = pltpu.roll(x, HEAD_DIM // 2, 1)
        return x * cos + jnp.where(first_half, ahead, behind) * sin

    def head_norm(x, g):
        ms = _split_dot(x * x, same_head) * (1.0 / HEAD_DIM)
        return (x * lax.rsqrt(ms + EPS)) * g

    for j in range(3):
        qa_ref[:, j * LANES:(j + 1) * LANES] = rope(head_norm(chunk(j), gq_ref[...]))
    ka_ref[...] = rope(head_norm(chunk(3), gk_ref[...]))
    va_ref[...] = chunk(4)
    for j in range(2):
        qi_ref[:, j * LANES:(j + 1) * LANES] = rope(chunk(5 + j))
    for j in range(3):
        qb_ref[:, j * LANES:(j + 1) * LANES] = chunk(7 + j)
        kb_ref[:, j * LANES:(j + 1) * LANES] = chunk(10 + j)
        vb_ref[:, j * LANES:(j + 1) * LANES] = chunk(13 + j)
    for j in range(2):
        u_ref[:, j * LANES:(j + 1) * LANES] = chunk(16 + j)
    kw = chunk(18)
    kiwi_ref[...] = jnp.where(lane < IDX_DIM, rope(kw), kw * (N_IDX_HEADS ** -0.5))


def _in_proj(x, g, w_perm, cos_t, sin_t, gq, gk, tm):
    rows, d = x.shape
    n = rows // tm
    table_tiles = cos_t.shape[0] // tm
    widths = (WIDTH_A, WIDTH_KV_A, WIDTH_KV_A, WIDTH_QI, WIDTH_B, WIDTH_B, WIDTH_B, WIDTH_C, LANES)

    def row(w):
        return pl.BlockSpec((tm, w), lambda i: (i, 0))

    table = pl.BlockSpec((tm, LANES), lambda i: (i % table_tiles, 0))
    return pl.pallas_call(
        _in_proj_body,
        grid=(n,),
        in_specs=[row(d), _const_spec(g, 1), _const_spec(w_perm, 1), table, table,
                  _const_spec(gq, 1), _const_spec(gk, 1)],
        out_specs=[row(w) for w in widths],
        out_shape=[jax.ShapeDtypeStruct((rows, w), F32) for w in widths],
        compiler_params=_params(1),
        name="in_proj",
    )(x, g, w_perm, cos_t, sin_t, gq, gk)


def _half_select(x, src_half, dst_half, lane):
    if src_half != dst_half:
        x = pltpu.roll(x, HEAD_DIM, 1)
    return jnp.where((lane // HEAD_DIM) == dst_half, x, 0.0)


def _dsa_prompt_body(qi_ref, kiwq_ref, qa_ref, kiwi_ref, ka_ref, va_ref, o_ref,
                     key_ref, bias_ref, lg_ref, acc_ref, *, topk):
    tq = qi_ref.shape[0]
    st = key_ref.shape[1]
    n_q = N_HEADS_A * tq
    i = pl.program_id(1)
    n_st = (i * tq + tq + st - 1) // st
    q_pos = i * tq + lax.broadcasted_iota(I32, (1, tq), 1)
    k_row = lax.broadcasted_iota(I32, (st, tq), 0)
    lane_q = lax.broadcasted_iota(I32, (tq, LANES), 1)

    def rows(s):
        return pl.ds(pl.multiple_of(s * st, st), st)

    def count(pred):
        def body(s, acc):
            return acc + _fold_rows(jnp.where(pred(key_ref[s]), 1.0, 0.0), jnp.add)
        acc = lax.fori_loop(0, n_st, body, jnp.zeros((SUBLANES, tq), F32))
        return jnp.sum(acc, axis=0, keepdims=True)

    kiwq_t = kiwq_ref[...].T
    w_row = [kiwq_t[IDX_DIM + h:IDX_DIM + h + 1, :] for h in range(N_IDX_HEADS)]
    qi = qi_ref[...]
    qi_h = [qi[:, h * IDX_DIM:(h + 1) * IDX_DIM].astype(BF16) for h in range(N_IDX_HEADS)]

    def score_tile(s, carry):
        ki = kiwi_ref[rows(s), 0:IDX_DIM].astype(BF16)
        sc = jnp.zeros((st, tq), F32)
        for h in range(N_IDX_HEADS):
            d = _dot_t(ki, qi_h[h]) * (IDX_DIM ** -0.5)
            sc = sc + jnp.maximum(d, 0.0) * w_row[h]
        key_ref[s] = jnp.where(s * st + k_row <= q_pos, _sort_key(sc), INT_MIN)
        return carry

    lax.fori_loop(0, n_st, score_tile, 0)

    def bit_step(b, thr):
        cand = thr + lax.shift_left(jnp.int32(1), 31 - b)
        cnt = count(lambda k: k >= cand)
        return jnp.where(cnt >= topk, cand, thr)

    thr = lax.fori_loop(0, 32, bit_step, jnp.full((1, tq), INT_MIN, I32))

    n_ties = topk - count(lambda k: k > thr)
    r = lax.broadcasted_iota(I32, (st, st), 0)
    c = lax.broadcasted_iota(I32, (st, st), 1)
    prefix_incl = (c <= r).astype(BF16)

    def mask_tile(s, seen):
        k = key_ref[s]
        eq = k == thr
        eq_f = jnp.where(eq, 1.0, 0.0)
        rank = _dot(prefix_incl, eq_f.astype(BF16)) + seen
        sel = ((k > thr) | (eq & (rank <= n_ties))) & (k != INT_MIN)
        bias_ref[s] = jnp.where(sel, 0.0, MASKED_LOGIT)
        return seen + jnp.sum(_fold_rows(eq_f, jnp.add), axis=0, keepdims=True)

    lax.fori_loop(0, n_st, mask_tile, jnp.zeros((1, tq), F32))

    qa = qa_ref[...]
    q_t = []
    for h in range(N_HEADS_A):
        chunk = qa[:, (h // 2) * LANES:(h // 2 + 1) * LANES]
        q_t.append(_half_select(chunk, h % 2, h // KV_GROUP, lane_q).T)
    q_t = jnp.concatenate(q_t, axis=1).astype(BF16)

    def logits_tile(s, mx):
        k = ka_ref[rows(s), :].astype(BF16)
        b = bias_ref[s]
        lg = _dot(k, q_t) * (HEAD_DIM ** -0.5) + jnp.concatenate([b] * N_HEADS_A, axis=1)
        lg_ref[s] = lg
        return jnp.maximum(mx, _fold_rows(lg, jnp.maximum))

    mx = lax.fori_loop(0, n_st, logits_tile, jnp.full((SUBLANES, n_q), MASKED_LOGIT, F32))
    m = jnp.max(mx, axis=0, keepdims=True)
    acc_ref[...] = jnp.zeros(acc_ref.shape, F32)

    def weights_tile(s, l_part):
        p = jnp.exp(lg_ref[s] - m)
        v_t = va_ref[rows(s), :].T.astype(BF16)
        acc_ref[...] += _dot(v_t, p.astype(BF16))
        return l_part + _fold_rows(p, jnp.add)

    l_part = lax.fori_loop(0, n_st, weights_tile, jnp.zeros((SUBLANES, n_q), F32))
    o_t = acc_ref[...] / jnp.sum(l_part, axis=0, keepdims=True)
    for cc in range(N_HEADS_A // 2):
        parts = []
        for h in (2 * cc, 2 * cc + 1):
            x = o_t[:, h * tq:(h + 1) * tq].T
            parts.append(_half_select(x, h // KV_GROUP, h % 2, lane_q))
        o_ref[:, cc * LANES:(cc + 1) * LANES] = parts[0] + parts[1]


def _dsa_prompt(qi, kiwi, qa, ka, va, batch, seq):
    tq = ATT_TILE
    st = KEY_TILE
    nq = seq // tq

    def q_spec(w):
        return pl.BlockSpec((tq, w), lambda b, i: (b * nq + i, 0))

    def k_spec(w):
        return pl.BlockSpec((seq, w), lambda b, i: (b, 0))

    return pl.pallas_call(
        functools.partial(_dsa_prompt_body, topk=float(min(TOPK_MAX, seq // 4))),
        grid=(batch, nq),
        in_specs=[q_spec(WIDTH_QI), q_spec(LANES), q_spec(WIDTH_A),
                  k_spec(LANES), k_spec(WIDTH_KV_A), k_spec(WIDTH_KV_A)],
        out_specs=q_spec(WIDTH_A),
        out_shape=jax.ShapeDtypeStruct((batch * seq, WIDTH_A), F32),
        scratch_shapes=[pltpu.VMEM((seq // st, st, tq), I32),
                        pltpu.VMEM((seq // st, st, tq), F32),
                        pltpu.VMEM((seq // st, st, N_HEADS_A * tq), F32),
                        pltpu.VMEM((LANES, N_HEADS_A * tq), F32)],
        compiler_params=_params(2),
        name="dsa_prompt",
    )(qi, kiwi, qa, kiwi, ka, va)


def _page_copies(pt_ref, step, slot, layer, caches, bufs, sem, pages, page):
    group = bufs[0].shape[1]
    out = []
    for g in range(group):
        for j, p in enumerate(pages):
            pg = pt_ref[step * group + g, p]
            for ci, (cache, buf) in enumerate(zip(caches, bufs)):
                out.append(pltpu.make_async_copy(
                    cache.at[layer, pg], buf.at[slot, g, :, pl.ds(j * page, page)],
                    sem.at[slot, ci, g * len(pages) + j]))
    return out


def _prefetch_pages(pt_ref, layer, caches, bufs, sem, pages, page):
    s = pl.program_id(0)
    slot = s % 2

    @pl.when(s == 0)
    def _():
        for cp in _page_copies(pt_ref, 0, 0, layer, caches, bufs, sem, pages, page):
            cp.start()

    @pl.when(s + 1 < pl.num_programs(0))
    def _():
        for cp in _page_copies(pt_ref, s + 1, 1 - slot, layer, caches, bufs, sem, pages, page):
            cp.start()

    for cp in _page_copies(pt_ref, s, slot, layer, caches, bufs, sem, pages, page):
        cp.wait()
    return slot


def _dsa_sample_body(pt_ref, qi_ref, kiwi_ref, qa_ref, ka_ref, va_ref, cidx_ref, cak_ref, cav_ref,
                     o_ref, ki_buf, ka_buf, va_buf, sem, key_ref, bias_ref,
                     *, layer, n_pages, page, topk, n_new):
    group, tq = qi_ref.shape[0], qi_ref.shape[1]
    past = n_pages * page
    total = past + page
    rows_all = group * tq
    new = slice(past, total)
    slot = _prefetch_pages(pt_ref, layer, (cidx_ref, cak_ref, cav_ref), (ki_buf, ka_buf, va_buf),
                           sem, tuple(range(n_pages)), page)
    lane_q = lax.broadcasted_iota(I32, (tq, LANES), 1)
    k_pos = lax.broadcasted_iota(I32, (tq, total), 1)
    q_pos = past + lax.broadcasted_iota(I32, (tq, 1), 0)

    for g in range(group):
        kw_t = _pad_rows(kiwi_ref[g], page).T
        ki_buf[slot, g, :, new] = kw_t[0:IDX_DIM, :]
        ka_buf[slot, g, :, new] = _pad_rows(ka_ref[g], page).T
        va_buf[slot, g, :, new] = _pad_rows(va_ref[g], page).T

    for g in range(group):
        qi = qi_ref[g]
        wi = kiwi_ref[g][:, IDX_DIM:IDX_DIM + N_IDX_HEADS]
        ki_t = ki_buf[slot, g].astype(BF16)
        sc = jnp.zeros((tq, total), F32)
        for h in range(N_IDX_HEADS):
            d = _dot(qi[:, h * IDX_DIM:(h + 1) * IDX_DIM].astype(BF16), ki_t) * (IDX_DIM ** -0.5)
            sc = sc + jnp.maximum(d, 0.0) * wi[:, h:h + 1]
        key_ref[g * tq:(g + 1) * tq, :] = jnp.where(k_pos <= q_pos, _sort_key(sc), INT_MIN)

    def bit_step(b, thr):
        cand = thr + lax.shift_left(jnp.int32(1), 31 - b)
        cnt = jnp.sum(jnp.where(key_ref[...] >= cand, 1.0, 0.0), axis=1, keepdims=True)
        return jnp.where(cnt >= topk, cand, thr)

    thr = lax.fori_loop(0, 32, bit_step, jnp.full((rows_all, 1), INT_MIN, I32))

    keys = key_ref[...]
    eq_f = jnp.where(keys == thr, 1.0, 0.0)
    n_ties = topk - jnp.sum(jnp.where(keys > thr, 1.0, 0.0), axis=1, keepdims=True)
    real_row = lax.broadcasted_iota(I32, (rows_all, 1), 0) % tq < n_new
    surplus = jnp.where(real_row, jnp.sum(eq_f, axis=1, keepdims=True) - n_ties, 0.0)
    bias_ref[...] = jnp.where((keys >= thr) & (keys != INT_MIN), 0.0, MASKED_LOGIT)

    @pl.when(jnp.max(surplus) > 0.0)
    def _():
        r = lax.broadcasted_iota(I32, (page, page), 0)
        c = lax.broadcasted_iota(I32, (page, page), 1)
        prefix_incl = (r <= c).astype(BF16)
        seen = jnp.zeros((rows_all, 1), F32)
        for kt in range(n_pages + 1):
            lanes = slice(kt * page, (kt + 1) * page)
            k = key_ref[:, lanes]
            eq = k == thr
            e = jnp.where(eq, 1.0, 0.0)
            rank = _dot(e.astype(BF16), prefix_incl) + seen
            sel = ((k > thr) | (eq & (rank <= n_ties))) & (k != INT_MIN)
            bias_ref[:, lanes] = jnp.where(sel, 0.0, MASKED_LOGIT)
            seen = seen + jnp.sum(e, axis=1, keepdims=True)

    for g in range(group):
        qa = qa_ref[g]
        q_rows = []
        for h in range(N_HEADS_A):
            chunk = qa[:, (h // 2) * LANES:(h // 2 + 1) * LANES]
            q_rows.append(_half_select(chunk, h % 2, h // KV_GROUP, lane_q))
        q_stack = jnp.concatenate(q_rows, axis=0).astype(BF16)
        b = bias_ref[g * tq:(g + 1) * tq, :]
        lg = (_dot(q_stack, ka_buf[slot, g].astype(BF16)) * (HEAD_DIM ** -0.5)
              + jnp.concatenate([b] * N_HEADS_A, axis=0))
        p = jnp.exp(lg - jnp.max(lg, axis=1, keepdims=True))
        o = _dot_t(p.astype(BF16), va_buf[slot, g].astype(BF16)) / jnp.sum(p, axis=1, keepdims=True)
        for cc in range(N_HEADS_A // 2):
            parts = []
            for h in (2 * cc, 2 * cc + 1):
                parts.append(_half_select(o[h * tq:(h + 1) * tq], h // KV_GROUP, h % 2, lane_q))
            o_ref[g, :, cc * LANES:(cc + 1) * LANES] = parts[0] + parts[1]


def _dsa_sample(page_table, qi, kiwi, qa, ka, va, cache_idx_t, cache_ak_t, cache_av_t, layer, n_new):
    nb, n_pages = page_table.shape
    page = cache_idx_t.shape[3]
    tq = SAMPLE_PAD
    group = SAMPLE_GROUP
    total = (n_pages + 1) * page

    def q_spec(w):
        return pl.BlockSpec((group, tq, w), lambda s, pt: (s, 0, 0))

    hbm = pl.BlockSpec(memory_space=pl.ANY)
    grid_spec = pltpu.PrefetchScalarGridSpec(
        num_scalar_prefetch=1,
        grid=(nb // group,),
        in_specs=[q_spec(WIDTH_QI), q_spec(LANES), q_spec(WIDTH_A), q_spec(WIDTH_KV_A), q_spec(WIDTH_KV_A),
                  hbm, hbm, hbm],
        out_specs=q_spec(WIDTH_A),
        scratch_shapes=[pltpu.VMEM((2, group, IDX_DIM, total), F32),
                        pltpu.VMEM((2, group, WIDTH_KV_A, total), F32),
                        pltpu.VMEM((2, group, WIDTH_KV_A, total), F32),
                        pltpu.SemaphoreType.DMA((2, 3, group * n_pages)),
                        pltpu.VMEM((group * tq, total), I32),
                        pltpu.VMEM((group * tq, total), F32)],
    )
    topk = float(min(TOPK_MAX, (n_pages * page + n_new) // 4))
    return pl.pallas_call(
        functools.partial(_dsa_sample_body, layer=layer, n_pages=n_pages, page=page, topk=topk, n_new=n_new),
        grid_spec=grid_spec,
        out_shape=jax.ShapeDtypeStruct((nb, tq, WIDTH_A), F32),
        compiler_params=_params(1),
        name="dsa_sample",
    )(page_table, qi, kiwi, qa, ka, va, cache_idx_t, cache_ak_t, cache_av_t)


def _softplus(z):
    return jnp.maximum(z, 0.0) + jnp.log1p(jnp.exp(-jnp.abs(z)))


def _sb_core(qb, q_pos, kt_last, k_tile, v_tile, carry_ref, acc_ref, o_ref, tk):
    tq = qb.shape[0]
    lane_q = lax.broadcasted_iota(I32, (tq, LANES), 1)
    lane_k = lax.broadcasted_iota(I32, (tq, tk), 1)
    r = lax.broadcasted_iota(I32, (tk, tk), 0)
    c = lax.broadcasted_iota(I32, (tk, tk), 1)
    suffix_incl = (r >= c).astype(BF16)
    q_pad = []
    for h in range(N_HEADS_B):
        chunk = qb[:, (h // 2) * LANES:(h // 2 + 1) * LANES]
        q_pad.append(jnp.where((lane_q // HEAD_DIM) == (h % 2), chunk, 0.0).astype(BF16))
    carry_ref[...] = jnp.zeros(carry_ref.shape, F32)
    acc_ref[...] = jnp.zeros(acc_ref.shape, F32)

    def cond(state):
        kt, worst = state
        return jnp.logical_and(kt >= 0, worst > F32_EXP_UNDERFLOW)

    def body(state):
        kt, _ = state
        strict = kt * tk + lane_k < q_pos
        worst = jnp.full((tq, 1), -jnp.inf, F32)
        for cc in range(N_HEADS_B // 2):
            k = k_tile(kt, cc).astype(BF16)
            v = v_tile(kt, cc).astype(BF16)
            for h in (2 * cc, 2 * cc + 1):
                z = _dot_t(q_pad[h], k) * (HEAD_DIM ** -0.5)
                log_rem = jnp.where(strict, -_softplus(z), 0.0)
                rev = _split_dot(log_rem, suffix_incl)
                carry = carry_ref[h]
                a = jnp.where(strict, jnp.exp(z + rev + carry), 0.0)
                acc_ref[h] = acc_ref[h] + _dot(a.astype(BF16), v)
                carry = carry + rev[:, 0:1]
                carry_ref[h] = carry
                worst = jnp.maximum(worst, carry)
        return kt - 1, jnp.max(worst)

    lax.while_loop(cond, body, (kt_last, jnp.float32(0.0)))
    for cc in range(N_HEADS_B // 2):
        o_ref[:, cc * LANES:(cc + 1) * LANES] = jnp.where(
            lane_q < HEAD_DIM, acc_ref[2 * cc], acc_ref[2 * cc + 1])


def _sb_prompt_body(qb_ref, kb_ref, vb_ref, o_ref, carry_ref, acc_ref):
    tq = qb_ref.shape[0]
    i = pl.program_id(1)
    q_pos = i * tq + lax.broadcasted_iota(I32, (tq, 1), 0)

    def tile(ref):
        return lambda kt, cc: ref[pl.ds(pl.multiple_of(kt * tq, tq), tq), cc * LANES:(cc + 1) * LANES]

    _sb_core(qb_ref[...], q_pos, i, tile(kb_ref), tile(vb_ref), carry_ref, acc_ref, o_ref, tq)


def _sb_prompt(qb, kb, vb, batch, seq):
    tq = ATT_TILE
    nq = seq // tq
    q_spec = pl.BlockSpec((tq, WIDTH_B), lambda b, i: (b * nq + i, 0))
    k_spec = pl.BlockSpec((seq, WIDTH_B), lambda b, i: (b, 0))
    return pl.pallas_call(
        _sb_prompt_body,
        grid=(batch, nq),
        in_specs=[q_spec, k_spec, k_spec],
        out_specs=q_spec,
        out_shape=jax.ShapeDtypeStruct((batch * seq, WIDTH_B), F32),
        scratch_shapes=[pltpu.VMEM((N_HEADS_B, tq, 1), F32), pltpu.VMEM((N_HEADS_B, tq, LANES), F32)],
        compiler_params=_params(2),
        name="sb_prompt",
    )(qb, kb, vb)


def _sb_sample_body(pt_ref, qb_ref, kb_ref, vb_ref, ck_ref, cv_ref, o_ref,
                    k_buf, v_buf, sem, kf_buf, vf_buf, sem_f, carry_ref, acc_ref,
                    *, layer, n_pages, page):
    group, tq = qb_ref.shape[0], qb_ref.shape[1]
    past = n_pages * page
    eager = tuple(range(n_pages - 1, n_pages - 1 - SB_EAGER_PAGES, -1))
    slot = _prefetch_pages(pt_ref, layer, (ck_ref, cv_ref), (k_buf, v_buf), sem, eager, page)
    s = pl.program_id(0)
    lane_q = lax.broadcasted_iota(I32, (tq, LANES), 1)
    lane_k = lax.broadcasted_iota(I32, (tq, page), 1)
    r = lax.broadcasted_iota(I32, (page, page), 0)
    c = lax.broadcasted_iota(I32, (page, page), 1)
    suffix_incl = (r >= c).astype(BF16)
    strict_new = lane_k < lax.broadcasted_iota(I32, (tq, page), 0)

    def visit(q_pad, k_t, v_t, strict, carry, acc):
        for cc in range(N_HEADS_B // 2):
            k = k_t(cc).astype(BF16)
            v = v_t(cc).astype(BF16)
            for h in (2 * cc, 2 * cc + 1):
                z = _dot(q_pad[h], k) * (HEAD_DIM ** -0.5)
                log_rem = -_softplus(z)
                if strict is not None:
                    log_rem = jnp.where(strict, log_rem, 0.0)
                rev = _split_dot(log_rem, suffix_incl)
                a = jnp.exp(z + rev + carry[h])
                if strict is not None:
                    a = jnp.where(strict, a, 0.0)
                acc[h] = acc[h] + _dot_t(a.astype(BF16), v)
                carry[h] = carry[h] + rev[:, 0:1]

    def rows(cc):
        return slice(cc * LANES, (cc + 1) * LANES)

    q_pads = []
    for g in range(group):
        qb = qb_ref[g]
        q_pad = []
        for h in range(N_HEADS_B):
            q_pad.append(jnp.where((lane_q // HEAD_DIM) == (h % 2), qb[:, rows(h // 2)], 0.0).astype(BF16))
        q_pads.append(q_pad)
        carry = [jnp.zeros((tq, 1), F32) for _ in range(N_HEADS_B)]
        acc = [jnp.zeros((tq, LANES), F32) for _ in range(N_HEADS_B)]
        kb = kb_ref[g]
        vb = vb_ref[g]
        visit(q_pad, lambda cc: _pad_rows(kb[:, rows(cc)], page).T, lambda cc: _pad_rows(vb[:, rows(cc)], page).T,
              strict_new, carry, acc)
        for j in range(SB_EAGER_PAGES):
            lanes = slice(j * page, (j + 1) * page)
            visit(q_pad, lambda cc: k_buf[slot, g, rows(cc), lanes], lambda cc: v_buf[slot, g, rows(cc), lanes],
                  None, carry, acc)
        for h in range(N_HEADS_B):
            carry_ref[g, h] = carry[h]
            acc_ref[g, h] = acc[h]

    for g in range(group):
        q_pad = q_pads[g]
        seq = s * group + g

        def worst_carry():
            w = carry_ref[g, 0]
            for h in range(1, N_HEADS_B):
                w = jnp.maximum(w, carry_ref[g, h])
            return jnp.max(w)

        def cond(state):
            p, worst = state
            return jnp.logical_and(p >= 0, worst > F32_EXP_UNDERFLOW)

        def body(state):
            p, _ = state
            pg = pt_ref[seq, p]
            ck = pltpu.make_async_copy(ck_ref.at[layer, pg], kf_buf, sem_f.at[0])
            cv = pltpu.make_async_copy(cv_ref.at[layer, pg], vf_buf, sem_f.at[1])
            ck.start()
            cv.start()
            ck.wait()
            cv.wait()
            carry = [carry_ref[g, h] for h in range(N_HEADS_B)]
            acc = [acc_ref[g, h] for h in range(N_HEADS_B)]
            visit(q_pad, lambda cc: kf_buf[rows(cc), :], lambda cc: vf_buf[rows(cc), :], None, carry, acc)
            for h in range(N_HEADS_B):
                carry_ref[g, h] = carry[h]
                acc_ref[g, h] = acc[h]
            return p - 1, worst_carry()

        lax.while_loop(cond, body, (jnp.int32(n_pages - 1 - SB_EAGER_PAGES), worst_carry()))
        for cc in range(N_HEADS_B // 2):
            o_ref[g, :, rows(cc)] = jnp.where(lane_q < HEAD_DIM, acc_ref[g, 2 * cc], acc_ref[g, 2 * cc + 1])


def _sb_sample(page_table, qb, kb, vb, cache_k_t, cache_v_t, layer):
    nb, n_pages = page_table.shape
    page = cache_k_t.shape[3]
    tq = SAMPLE_PAD
    group = SAMPLE_GROUP
    q_spec = pl.BlockSpec((group, tq, WIDTH_B), lambda s, pt: (s, 0, 0))
    hbm = pl.BlockSpec(memory_space=pl.ANY)
    grid_spec = pltpu.PrefetchScalarGridSpec(
        num_scalar_prefetch=1,
        grid=(nb // group,),
        in_specs=[q_spec, q_spec, q_spec, hbm, hbm],
        out_specs=q_spec,
        scratch_shapes=[pltpu.VMEM((2, group, WIDTH_B, SB_EAGER_PAGES * page), F32),
                        pltpu.VMEM((2, group, WIDTH_B, SB_EAGER_PAGES * page), F32),
                        pltpu.SemaphoreType.DMA((2, 2, group * SB_EAGER_PAGES)),
                        pltpu.VMEM((WIDTH_B, page), F32),
                        pltpu.VMEM((WIDTH_B, page), F32),
                        pltpu.SemaphoreType.DMA((2,)),
                        pltpu.VMEM((group, N_HEADS_B, tq, 1), F32),
                        pltpu.VMEM((group, N_HEADS_B, tq, LANES), F32)],
    )
    return pl.pallas_call(
        functools.partial(_sb_sample_body, layer=layer, n_pages=n_pages, page=page),
        grid_spec=grid_spec,
        out_shape=jax.ShapeDtypeStruct((nb, tq, WIDTH_B), F32),
        compiler_params=_params(1),
        name="sb_sample",
    )(page_table, qb, kb, vb, cache_k_t, cache_v_t)


def _mix_body(h_ref, oa_ref, ob_ref, u_ref, halo_ref, woa_ref, wob_ref, woc_ref, wpool_ref, pscale_ref,
              o_ref, ext_a, ext_b, *, unit, tiles_per_seq, pos0, zero_first_halo):
    tm = h_ref.shape[0]
    i = pl.program_id(0)
    tile_in_seq = i % tiles_per_seq
    lo = 8 * unit
    base = 24 * unit
    n = base + tm
    u = u_ref[...]
    halo = halo_ref[...]
    if zero_first_halo:
        halo = jnp.where(tile_in_seq == 0, 0.0, halo)
    zeros = jnp.zeros((lo, WIDTH_C), F32)
    ext_a[pl.ds(0, lo), :] = zeros
    ext_b[pl.ds(0, lo), :] = zeros
    ext_a[pl.ds(lo, 16 * unit), :] = halo
    ext_a[pl.ds(base, tm), :] = u

    def doubled(src, shift):
        return src[pl.ds(lo, n - lo), :] + src[pl.ds(lo - shift * unit, n - lo), :]

    cur = 16 * unit
    s2 = doubled(ext_a, 1)
    ext_b[pl.ds(lo, n - lo), :] = s2
    s4 = doubled(ext_b, 2)
    ext_a[pl.ds(lo, n - lo), :] = s4
    s8 = doubled(ext_a, 4)
    ext_b[pl.ds(lo, n - lo), :] = s8
    s16 = doubled(ext_b, 8)
    lane = lax.broadcasted_iota(I32, (tm, WIDTH_C), 1)
    group = lane // POOL_GROUP_DIM
    sums = jnp.where(group == 0, s2[cur:], jnp.where(group == 1, s4[cur:],
                     jnp.where(group == 2, s8[cur:], s16[cur:])))
    window = jnp.where(group == 0, POOL_WINDOWS[0], jnp.where(group == 1, POOL_WINDOWS[1],
                       jnp.where(group == 2, POOL_WINDOWS[2], POOL_WINDOWS[3])))
    token = tile_in_seq * (tm // unit) + lax.broadcasted_iota(I32, (tm, WIDTH_C), 0) // unit
    cnt = jnp.minimum(window, pos0 + token + 1).astype(F32)
    pooled = (sums / cnt - u).astype(BF16)
    y = (_dot(pooled, wpool_ref[...]) * pscale_ref[...]).astype(BF16)
    mix = (_dot(oa_ref[...].astype(BF16), woa_ref[...]) + _dot(ob_ref[...].astype(BF16), wob_ref[...])
           + _dot(y, woc_ref[...]))
    o_ref[...] = h_ref[...] + mix


def _mix(h, oa, ob, u, halo, woa, wob, woc, wpool, pscale, *, unit, tiles_per_seq, pos0, prompt):
    rows, d = h.shape
    tm = DENSE_TILE
    n = rows // tm

    def row(w):
        return pl.BlockSpec((tm, w), lambda i: (i, 0))

    if prompt:
        per = tm // (16 * unit)
        halo_spec = pl.BlockSpec((16 * unit, WIDTH_C), lambda i: (jnp.maximum(i * per - 1, 0), 0))
    else:
        halo_spec = pl.BlockSpec((16 * unit, WIDTH_C), lambda i: (i, 0))
    consts = (woa, wob, woc, wpool, pscale)
    return pl.pallas_call(
        functools.partial(_mix_body, unit=unit, tiles_per_seq=tiles_per_seq, pos0=pos0, zero_first_halo=prompt),
        grid=(n,),
        in_specs=[row(d), row(WIDTH_A), row(WIDTH_B), row(WIDTH_C), halo_spec] + [_const_spec(a, 1) for a in consts],
        out_specs=row(d),
        out_shape=jax.ShapeDtypeStruct((rows, d), F32),
        scratch_shapes=[pltpu.VMEM((24 * unit + tm, WIDTH_C), F32), pltpu.VMEM((24 * unit + tm, WIDTH_C), F32)],
        compiler_params=_params(1),
        name="mix",
    )(h, oa, ob, u, halo, *consts)


def _ffn_body(h_ref, p_ref, pre_ref, gffn_ref, wup_ref, cw_ref, cb_ref, wdn_ref, gple_ref, wpg_ref, wpp_ref,
              o_ref, tail_ref, ext_ref, *, unit, tiles_per_seq):
    tm = h_ref.shape[0]
    halo = pre_ref.shape[0]
    dff = cw_ref.shape[1]
    i = pl.program_id(0)
    h = h_ref[...]
    up = _dot(_rms(h, gffn_ref[...]).astype(BF16), wup_ref[...])
    gate = up[:, :dff]
    val = up[:, dff:]

    @pl.when(i % tiles_per_seq == 0)
    def _():
        ext_ref[pl.ds(0, halo), :] = pre_ref[...]

    ext_ref[pl.ds(halo, tm), :] = gate
    conv = cb_ref[...] + ext_ref[pl.ds(halo - 2 * unit, tm), :] * cw_ref[0:1, :]
    conv = conv + ext_ref[pl.ds(halo - unit, tm), :] * cw_ref[1:2, :]
    conv = conv + gate * cw_ref[2:3, :]
    act = (conv * _sigmoid(conv) * val).astype(BF16)
    h2 = h + _dot(act, wdn_ref[...])
    tail = ext_ref[pl.ds(tm, halo), :]
    tail_ref[...] = tail
    ext_ref[pl.ds(0, halo), :] = tail
    gt = _sigmoid(_dot(_rms(h2, gple_ref[...]).astype(BF16), wpg_ref[...]))
    o_ref[...] = h2 + gt * _dot(p_ref[...].astype(BF16), wpp_ref[...])


def _ffn(h, p, pre, gffn, wup, cw, cb, wdn, gple, wpg, wpp, *, unit, tiles_per_seq, pre_per_seq):
    rows, d = h.shape
    tm = DENSE_TILE
    n = rows // tm
    halo = max(SUBLANES, (CONV_WIDTH - 1) * unit)
    dff = cw.shape[1]

    def row(w):
        return pl.BlockSpec((tm, w), lambda i: (i, 0))

    if pre_per_seq:
        pre_spec = pl.BlockSpec((halo, dff), lambda i: (i // tiles_per_seq, 0))
    else:
        pre_spec = pl.BlockSpec((halo, dff), lambda i: (0, 0))
    tail_spec = pl.BlockSpec((halo, dff), lambda i: (i // tiles_per_seq, 0))
    consts = (gffn, wup, cw, cb, wdn, gple, wpg, wpp)
    return pl.pallas_call(
        functools.partial(_ffn_body, unit=unit, tiles_per_seq=tiles_per_seq),
        grid=(n,),
        in_specs=[row(d), row(p.shape[1]), pre_spec] + [_const_spec(a, 1) for a in consts],
        out_specs=[row(d), tail_spec],
        out_shape=[jax.ShapeDtypeStruct((rows, d), F32),
                   jax.ShapeDtypeStruct((n // tiles_per_seq * halo, dff), F32)],
        scratch_shapes=[pltpu.VMEM((halo + tm, dff), F32)],
        compiler_params=_params(1),
        name="ffn",
    )(h, p, pre, *consts)


def _rope_tables(pos):
    half = HEAD_DIM // 2
    inv = ROPE_THETA ** (-jnp.arange(half, dtype=F32) / half)
    ang = pos.astype(F32)[:, None] * inv[None, :]
    cos = jnp.cos(ang)
    sin = jnp.sin(ang)
    cos_t = jnp.tile(cos, (1, LANES // half))
    sin_t = jnp.tile(jnp.concatenate([-sin, sin], axis=1), (1, LANES // HEAD_DIM))
    return cos_t, sin_t


def _matmul_weights(w_in, w_pool, w_out, w_up, w_down, w_ple_gate, w_ple_proj):
    starts = [0]
    for w in IN_SPLITS:
        starts.append(starts[-1] + w)
    w_perm = jnp.concatenate([w_in[:, :, starts[j]:starts[j + 1]] for j in _PERM_ORDER], axis=2)
    w_perm = jnp.pad(w_perm, ((0, 0), (0, 0), (0, _PERM_WIDTH - w_perm.shape[2])))
    groups = len(POOL_WINDOWS)
    eye = jnp.eye(groups, dtype=F32)
    wpool = (w_pool[:, :, :, None, :] * eye[None, :, None, :, None]).reshape(
        w_pool.shape[0], groups * POOL_GROUP_DIM, groups * POOL_GROUP_DIM)
    return tuple(_to_bf16(w) for w in (w_perm, wpool, w_out, w_up, w_down, w_ple_gate, w_ple_proj))


def _to_token_major(x, n_tok):
    nb, _, w = x.shape
    x = x[:, :n_tok].reshape(nb // SAMPLE_UNIT, SAMPLE_UNIT, n_tok, w)
    return x.transpose(0, 2, 1, 3).reshape(nb * n_tok, w)


def _state_token_major(x):
    nb, n_tok, w = x.shape
    return x.reshape(nb // SAMPLE_UNIT, SAMPLE_UNIT, n_tok, w).transpose(0, 2, 1, 3)


def _from_token_major(x, nb, n_tok):
    w = x.shape[1]
    x = x.reshape(nb // SAMPLE_UNIT, n_tok, SAMPLE_UNIT, w).transpose(0, 2, 1, 3)
    return x.reshape(nb, n_tok, w)


def kernel(x_prompt, x_sample, cache_a_k, cache_a_v, cache_idx_k, cache_b_k, cache_b_v, state_pool, state_conv, page_table, p_prompt, p_sample, g_attn, w_in, g_q_a, g_k_a, w_pool, pool_scale, w_out, g_ffn, w_up, conv_w, conv_b, w_down, g_ple, w_ple_gate, w_ple_proj):
    batch, seq, d = x_prompt.shape
    nb, n_new, _ = x_sample.shape
    depth, n_pool, page = cache_idx_k.shape[:3]
    n_pages = page_table.shape[1]
    past = n_pages * page
    dff = conv_w.shape[2]
    assert seq % ROW_TILE == 0 and seq % DENSE_TILE == 0 and seq % KEY_TILE == 0 and KEY_TILE % ATT_TILE == 0
    assert nb % SAMPLE_UNIT == 0 and nb % SAMPLE_GROUP == 0 and n_new <= SAMPLE_PAD and page == LANES
    assert (n_new * SAMPLE_UNIT) == DENSE_TILE and (nb * SAMPLE_PAD) % ROW_TILE == 0
    assert n_pages >= SB_EAGER_PAGES

    cos_p, sin_p = _rope_tables(jnp.arange(seq, dtype=I32))
    cos_s, sin_s = _rope_tables(past + jnp.arange(nb * SAMPLE_PAD, dtype=I32) % SAMPLE_PAD)
    cidx_t = cache_idx_k.transpose(0, 1, 3, 2)
    cak_t = cache_a_k.transpose(0, 1, 3, 4, 2).reshape(depth, n_pool, WIDTH_KV_A, page)
    cav_t = cache_a_v.transpose(0, 1, 3, 4, 2).reshape(depth, n_pool, WIDTH_KV_A, page)
    cbk_t = cache_b_k.transpose(0, 1, 3, 4, 2).reshape(depth, n_pool, WIDTH_B, page)
    cbv_t = cache_b_v.transpose(0, 1, 3, 4, 2).reshape(depth, n_pool, WIDTH_B, page)
    w_perm, wpool, wo, wup, wdn, wpg, wpp = _matmul_weights(
        w_in, w_pool, w_out, w_up, w_down, w_ple_gate, w_ple_proj)

    h_p = x_prompt.reshape(batch * seq, d)
    h_s = jnp.pad(x_sample, ((0, 0), (0, SAMPLE_PAD - n_new), (0, 0))).reshape(nb * SAMPLE_PAD, d)
    zero_pre = jnp.zeros((SUBLANES, dff), F32)
    rows_p, rows_s = [], []
    for l in range(depth):
        gq = jnp.tile(g_q_a[l], LANES // HEAD_DIM)[None, :]
        gk = jnp.tile(g_k_a[l], LANES // HEAD_DIM)[None, :]
        proj_w = (g_attn[l][None, :], w_perm[l])
        ffn_w = (g_ffn[l][None, :], wup[l], conv_w[l], conv_b[l][None, :], wdn[l], g_ple[l][None, :], wpg[l], wpp[l])
        mix_w = (wo[l, :WIDTH_A], wo[l, WIDTH_A:WIDTH_A + WIDTH_B], wo[l, WIDTH_A + WIDTH_B:],
                 wpool[l], pool_scale[l][None, :])

        qa, ka, va, qi, qb, kb, vb, u, kiwi = _in_proj(h_p, *proj_w, cos_p, sin_p, gq, gk, ROW_TILE)
        o_a = _dsa_prompt(qi, kiwi, qa, ka, va, batch, seq)
        o_b = _sb_prompt(qb, kb, vb, batch, seq)
        h1 = _mix(h_p, o_a, o_b, u, u, *mix_w, unit=1, tiles_per_seq=seq // DENSE_TILE, pos0=0, prompt=True)
        h_p, tail = _ffn(h1, p_prompt[l].reshape(batch * seq, -1), zero_pre, *ffn_w,
                         unit=1, tiles_per_seq=seq // DENSE_TILE, pre_per_seq=False)
        rows_p.append((
            ka.reshape(batch, seq, N_KV_A, HEAD_DIM), va.reshape(batch, seq, N_KV_A, HEAD_DIM),
            kiwi[:, :IDX_DIM].reshape(batch, seq, IDX_DIM),
            kb.reshape(batch, seq, N_HEADS_B, HEAD_DIM), vb.reshape(batch, seq, N_HEADS_B, HEAD_DIM),
            u.reshape(batch, seq, WIDTH_C)[:, seq - POOL_STATE:],
            tail.reshape(batch, SUBLANES, dff)[:, SUBLANES - (CONV_WIDTH - 1):]))

        qa, ka, va, qi, qb, kb, vb, u, kiwi = _in_proj(h_s, *proj_w, cos_s, sin_s, gq, gk, ROW_TILE)

        def nat(x):
            return x.reshape(nb, SAMPLE_PAD, x.shape[1])

        o_a = _dsa_sample(page_table, nat(qi), nat(kiwi), nat(qa), nat(ka), nat(va),
                          cidx_t, cak_t, cav_t, l, n_new)
        o_b = _sb_sample(page_table, nat(qb), nat(kb), nat(vb), cbk_t, cbv_t, l)
        u_nat = nat(u)[:, :n_new]
        pool_state = _state_token_major(state_pool[l])
        halo = jnp.pad(pool_state, ((0, 0), (1, 0), (0, 0), (0, 0))).reshape(-1, WIDTH_C)
        h1 = _mix(_to_token_major(nat(h_s), n_new), _to_token_major(o_a, n_new), _to_token_major(o_b, n_new),
                  _to_token_major(nat(u), n_new), halo, *mix_w,
                  unit=SAMPLE_UNIT, tiles_per_seq=1, pos0=past, prompt=False)
        pre = _state_token_major(state_conv[l]).reshape(-1, dff)
        h3, tail = _ffn(h1, _to_token_major(p_sample[l], n_new), pre, *ffn_w,
                        unit=SAMPLE_UNIT, tiles_per_seq=1, pre_per_seq=True)
        h_s_nat = _from_token_major(h3, nb, n_new)
        h_s = jnp.pad(h_s_nat, ((0, 0), (0, SAMPLE_PAD - n_new), (0, 0))).reshape(nb * SAMPLE_PAD, d)
        conv_state = tail.reshape(nb // SAMPLE_UNIT, CONV_WIDTH - 1, SAMPLE_UNIT, dff)
        conv_state = conv_state.transpose(0, 2, 1, 3).reshape(nb, CONV_WIDTH - 1, dff)
        rows_s.append((
            nat(ka)[:, :n_new].reshape(nb, n_new, N_KV_A, HEAD_DIM),
            nat(va)[:, :n_new].reshape(nb, n_new, N_KV_A, HEAD_DIM),
            nat(kiwi)[:, :n_new, :IDX_DIM],
            nat(kb)[:, :n_new].reshape(nb, n_new, N_HEADS_B, HEAD_DIM),
            nat(vb)[:, :n_new].reshape(nb, n_new, N_HEADS_B, HEAD_DIM),
            jnp.concatenate([state_pool[l], u_nat], axis=1)[:, n_new:],
            conv_state))

    def stack(rows, j):
        return jnp.stack([r[j] for r in rows], axis=0)

    y_p = h_p.reshape(batch, seq, d)
    y_s = h_s.reshape(nb, SAMPLE_PAD, d)[:, :n_new]
    return (y_p, y_s) + tuple(stack(rows_p, j) for j in range(7)) + tuple(stack(rows_s, j) for j in range(7))
```

```python
import functools

import jax
import jax.numpy as jnp
from jax import lax
from jax.experimental import pallas as pl
from jax.experimental.pallas import tpu as pltpu

F32 = jnp.float32
BF16 = jnp.bfloat16
I32 = jnp.int32

HEAD_DIM = 64
N_HEADS_A = 6
N_KV_A = 2
N_HEADS_B = 6
N_IDX_HEADS = 4
IDX_DIM = 64
TOPK_MAX = 256
POOL_WINDOWS = (2, 4, 8, 16)
POOL_GROUP_DIM = 64
POOL_STATE = 15
CONV_WIDTH = 3
ROPE_THETA = 10000.0
EPS = 1e-6
WIDTH_A = N_HEADS_A * HEAD_DIM
WIDTH_KV_A = N_KV_A * HEAD_DIM
WIDTH_QI = N_IDX_HEADS * IDX_DIM
WIDTH_B = N_HEADS_B * HEAD_DIM
WIDTH_C = len(POOL_WINDOWS) * POOL_GROUP_DIM
IN_SPLITS = (WIDTH_A, WIDTH_KV_A, WIDTH_KV_A, WIDTH_QI, IDX_DIM, N_IDX_HEADS,
             WIDTH_B, WIDTH_B, WIDTH_B, WIDTH_C)
KV_GROUP = N_HEADS_A // N_KV_A

LANES = 128
SUBLANES = 8
VMEM_LIMIT_BYTES = 56 * 1024 * 1024

INT_MIN = -2 ** 31
MASKED_LOGIT = -1e30
F32_EXP_UNDERFLOW = -104.0

ROW_TILE = 512
ATT_TILE = 128
KEY_TILE = 256
DENSE_TILE = 512
FFN_CHUNK = 1024
CAST_TILE = 256
SAMPLE_PAD = 8
SAMPLE_UNIT = 128
SAMPLE_GROUP = 4
SB_EAGER_PAGES = 2

_PERM_ORDER = (0, 1, 2, 3, 6, 7, 8, 9, 4, 5)
_PERM_WIDTH = 19 * LANES


def _params(n_axes):
    return pltpu.CompilerParams(
        dimension_semantics=("arbitrary",) * n_axes,
        vmem_limit_bytes=VMEM_LIMIT_BYTES)


def _const_spec(a, n_grid):
    zeros = (0,) * a.ndim
    if n_grid == 1:
        return pl.BlockSpec(a.shape, lambda i: zeros, pipeline_mode=pl.Buffered(1))
    return pl.BlockSpec(a.shape, lambda b, i: zeros, pipeline_mode=pl.Buffered(1))


def _rms(x, g):
    r = lax.rsqrt(jnp.mean(x * x, axis=-1, keepdims=True) + EPS)
    return (x * r) * g


def _sigmoid(x):
    return 1.0 / (1.0 + jnp.exp(-x))


def _dot(a, b):
    return jnp.dot(a, b, preferred_element_type=F32)


def _dot_t(a, b):
    return lax.dot_general(a, b, (((1,), (1,)), ((), ())), preferred_element_type=F32)


def _split_dot(x, m):
    hi = x.astype(BF16)
    lo = (x - hi.astype(F32)).astype(BF16)
    return _dot(hi, m) + _dot(lo, m)


def _fold_rows(x, op):
    parts = [x[j * SUBLANES:(j + 1) * SUBLANES] for j in range(x.shape[0] // SUBLANES)]
    while len(parts) > 1:
        nxt = [op(parts[j], parts[j + 1]) for j in range(0, len(parts) - 1, 2)]
        if len(parts) % 2:
            nxt.append(parts[-1])
        parts = nxt
    return parts[0]


def _sort_key(s):
    bits = lax.bitcast_convert_type(s, I32)
    key = jnp.where(bits < 0, bits ^ 0x7FFFFFFF, bits)
    return jnp.where(s == 0.0, 0, key)


def _pad_rows(x, rows):
    return jnp.concatenate([x, jnp.zeros((rows - x.shape[0], x.shape[1]), x.dtype)], axis=0)


def _cast_body(x_ref, o_ref):
    o_ref[...] = x_ref[...].astype(BF16)


def _to_bf16(w):
    depth, rows, cols = w.shape
    tr = CAST_TILE if rows % CAST_TILE == 0 else rows
    spec = pl.BlockSpec((None, tr, cols), lambda l, i: (l, i, 0))
    return pl.pallas_call(
        _cast_body,
        grid=(depth, rows // tr),
        in_specs=[spec],
        out_specs=spec,
        out_shape=jax.ShapeDtypeStruct(w.shape, BF16),
        compiler_params=_params(2),
        name="cast_bf16",
    )(w)


def _in_proj_body(x_ref, g_ref, w_ref, cos_ref, sin_ref, gq_ref, gk_ref,
                  qa_ref, ka_ref, va_ref, qi_ref, qb_ref, kb_ref, vb_ref, u_ref, kiwi_ref):
    tm = x_ref.shape[0]
    a = _rms(x_ref[...], g_ref[...]).astype(BF16)
    proj = _dot(a, w_ref[...])
    cos = cos_ref[...]
    sin = sin_ref[...]
    lane = lax.broadcasted_iota(I32, (tm, LANES), 1)
    first_half = (lane % HEAD_DIM) < (HEAD_DIM // 2)
    r = lax.broadcasted_iota(I32, (LANES, LANES), 0) // HEAD_DIM
    c = lax.broadcasted_iota(I32, (LANES, LANES), 1) // HEAD_DIM
    same_head = (r == c).astype(BF16)

    def chunk(j):
        return proj[:, j * LANES:(j + 1) * LANES]

    def rope(x):
        ahead = pltpu.roll(x, LANES - HEAD_DIM // 2, 1)
        behind = pltpu.roll(x, HEAD_DIM // 2, 1)
        return x * cos + jnp.where(first_half, ahead, behind) * sin

    def head_norm(x, g):
        ms = _split_dot(x * x, same_head) * (1.0 / HEAD_DIM)
        return (x * lax.rsqrt(ms + EPS)) * g

    for j in range(3):
        qa_ref[:, j * LANES:(j + 1) * LANES] = rope(head_norm(chunk(j), gq_ref[...]))
    ka_ref[...] = rope(head_norm(chunk(3), gk_ref[...]))
    va_ref[...] = chunk(4)
    for j in range(2):
        qi_ref[:, j * LANES:(j + 1) * LANES] = rope(chunk(5 + j))
    for j in range(3):
        qb_ref[:, j * LANES:(j + 1) * LANES] = chunk(7 + j)
        kb_ref[:, j * LANES:(j + 1) * LANES] = chunk(10 + j)
        vb_ref[:, j * LANES:(j + 1) * LANES] = chunk(13 + j)
    for j in range(2):
        u_ref[:, j * LANES:(j + 1) * LANES] = chunk(16 + j)
    kw = chunk(18)
    kiwi_ref[...] = jnp.where(lane < IDX_DIM, rope(kw), kw * (N_IDX_HEADS ** -0.5))


def _in_proj(x, g, w_perm, cos_t, sin_t, gq, gk, tm):
    rows, d = x.shape
    n = rows // tm
    table_tiles = cos_t.shape[0] // tm
    widths = (WIDTH_A, WIDTH_KV_A, WIDTH_KV_A, WIDTH_QI, WIDTH_B, WIDTH_B, WIDTH_B, WIDTH_C, LANES)

    def row(w):
        return pl.BlockSpec((tm, w), lambda i: (i, 0))

    table = pl.BlockSpec((tm, LANES), lambda i: (i % table_tiles, 0))
    return pl.pallas_call(
        _in_proj_body,
        grid=(n,),
        in_specs=[row(d), _const_spec(g, 1), _const_spec(w_perm, 1), table, table,
                  _const_spec(gq, 1), _const_spec(gk, 1)],
        out_specs=[row(w) for w in widths],
        out_shape=[jax.ShapeDtypeStruct((rows, w), F32) for w in widths],
        compiler_params=_params(1),
        name="in_proj",
    )(x, g, w_perm, cos_t, sin_t, gq, gk)


def _half_select(x, src_half, dst_half, lane):
    if src_half != dst_half:
        x = pltpu.roll(x, HEAD_DIM, 1)
    return jnp.where((lane // HEAD_DIM) == dst_half, x, 0.0)


def _dsa_prompt_body(qi_ref, kiwq_ref, qa_ref, kiwi_ref, ka_ref, va_ref, o_ref,
                     key_ref, bias_ref, lg_ref, acc_ref, *, topk):
    tq = qi_ref.shape[0]
    st = key_ref.shape[1]
    n_q = N_HEADS_A * tq
    i = pl.program_id(1)
    n_st = (i * tq + tq + st - 1) // st
    q_pos = i * tq + lax.broadcasted_iota(I32, (1, tq), 1)
    k_row = lax.broadcasted_iota(I32, (st, tq), 0)
    lane_q = lax.broadcasted_iota(I32, (tq, LANES), 1)

    def rows(s):
        return pl.ds(pl.multiple_of(s * st, st), st)

    def count(pred):
        def body(s, acc):
            return acc + _fold_rows(jnp.where(pred(key_ref[s]), 1.0, 0.0), jnp.add)
        acc = lax.fori_loop(0, n_st, body, jnp.zeros((SUBLANES, tq), F32))
        return jnp.sum(acc, axis=0, keepdims=True)

    kiwq_t = kiwq_ref[...].T
    w_row = [kiwq_t[IDX_DIM + h:IDX_DIM + h + 1, :] for h in range(N_IDX_HEADS)]
    qi = qi_ref[...]
    qi_t = [qi[:, j * LANES:(j + 1) * LANES].T for j in range(WIDTH_QI // LANES)]
    qi_t = jnp.concatenate([qi_t[h // 2][(h % 2) * IDX_DIM:(h % 2 + 1) * IDX_DIM] for h in range(N_IDX_HEADS)],
                           axis=1).astype(BF16)

    def score_tile(s, carry):
        ki = kiwi_ref[rows(s), 0:IDX_DIM].astype(BF16)
        d = _dot(ki, qi_t) * (IDX_DIM ** -0.5)
        sc = jnp.zeros((st, tq), F32)
        for h in range(N_IDX_HEADS):
            sc = sc + jnp.maximum(d[:, h * tq:(h + 1) * tq], 0.0) * w_row[h]
        key_ref[s] = jnp.where(s * st + k_row <= q_pos, _sort_key(sc), INT_MIN)
        return carry

    lax.fori_loop(0, n_st, score_tile, 0)

    def bit_step(b, thr):
        cand = thr + lax.shift_left(jnp.int32(1), 31 - b)
        cnt = count(lambda k: k >= cand)
        return jnp.where(cnt >= topk, cand, thr)

    thr = lax.fori_loop(0, 32, bit_step, jnp.full((1, tq), INT_MIN, I32))

    n_ties = topk - count(lambda k: k > thr)
    surplus = jnp.where(thr == INT_MIN, 0.0, count(lambda k: k == thr) - n_ties)
    has_surplus = jnp.max(surplus) > 0.0

    @pl.when(jnp.logical_not(has_surplus))
    def _():
        def mask_tile(s, carry):
            k = key_ref[s]
            bias_ref[s] = jnp.where((k >= thr) & (k != INT_MIN), 0.0, MASKED_LOGIT)
            return carry

        lax.fori_loop(0, n_st, mask_tile, 0)

    @pl.when(has_surplus)
    def _():
        r = lax.broadcasted_iota(I32, (st, st), 0)
        c = lax.broadcasted_iota(I32, (st, st), 1)
        prefix_incl = (c <= r).astype(BF16)

        def mask_tile(s, seen):
            k = key_ref[s]
            eq = k == thr
            eq_f = jnp.where(eq, 1.0, 0.0)
            rank = _dot(prefix_incl, eq_f.astype(BF16)) + seen
            sel = ((k > thr) | (eq & (rank <= n_ties))) & (k != INT_MIN)
            bias_ref[s] = jnp.where(sel, 0.0, MASKED_LOGIT)
            return seen + jnp.sum(_fold_rows(eq_f, jnp.add), axis=0, keepdims=True)

        lax.fori_loop(0, n_st, mask_tile, jnp.zeros((1, tq), F32))

    qa = qa_ref[...]
    q_t = []
    for h in range(N_HEADS_A):
        chunk = qa[:, (h // 2) * LANES:(h // 2 + 1) * LANES]
        q_t.append(_half_select(chunk, h % 2, h // KV_GROUP, lane_q).T)
    q_t = jnp.concatenate(q_t, axis=1).astype(BF16)

    def logits_tile(s, mx):
        k = ka_ref[rows(s), :].astype(BF16)
        b = bias_ref[s]
        lg = _dot(k, q_t) * (HEAD_DIM ** -0.5) + jnp.concatenate([b] * N_HEADS_A, axis=1)
        lg_ref[s] = lg
        return jnp.maximum(mx, _fold_rows(lg, jnp.maximum))

    mx = lax.fori_loop(0, n_st, logits_tile, jnp.full((SUBLANES, n_q), MASKED_LOGIT, F32))
    m = jnp.max(mx, axis=0, keepdims=True)
    acc_ref[...] = jnp.zeros(acc_ref.shape, F32)

    def weights_tile(s, l_part):
        p = jnp.exp(lg_ref[s] - m)
        v_t = va_ref[rows(s), :].T.astype(BF16)
        acc_ref[...] += _dot(v_t, p.astype(BF16))
        return l_part + _fold_rows(p, jnp.add)

    l_part = lax.fori_loop(0, n_st, weights_tile, jnp.zeros((SUBLANES, n_q), F32))
    o_t = acc_ref[...] / jnp.sum(l_part, axis=0, keepdims=True)
    for cc in range(N_HEADS_A // 2):
        parts = []
        for h in (2 * cc, 2 * cc + 1):
            x = o_t[:, h * tq:(h + 1) * tq].T
            parts.append(_half_select(x, h // KV_GROUP, h % 2, lane_q))
        o_ref[:, cc * LANES:(cc + 1) * LANES] = parts[0] + parts[1]


def _dsa_prompt(qi, kiwi, qa, ka, va, batch, seq):
    tq = ATT_TILE
    st = KEY_TILE
    nq = seq // tq

    def q_spec(w):
        return pl.BlockSpec((tq, w), lambda b, i: (b * nq + i, 0))

    def k_spec(w):
        return pl.BlockSpec((seq, w), lambda b, i: (b, 0))

    return pl.pallas_call(
        functools.partial(_dsa_prompt_body, topk=float(min(TOPK_MAX, seq // 4))),
        grid=(batch, nq),
        in_specs=[q_spec(WIDTH_QI), q_spec(LANES), q_spec(WIDTH_A),
                  k_spec(LANES), k_spec(WIDTH_KV_A), k_spec(WIDTH_KV_A)],
        out_specs=q_spec(WIDTH_A),
        out_shape=jax.ShapeDtypeStruct((batch * seq, WIDTH_A), F32),
        scratch_shapes=[pltpu.VMEM((seq // st, st, tq), I32),
                        pltpu.VMEM((seq // st, st, tq), F32),
                        pltpu.VMEM((seq // st, st, N_HEADS_A * tq), F32),
                        pltpu.VMEM((LANES, N_HEADS_A * tq), F32)],
        compiler_params=_params(2),
        name="dsa_prompt",
    )(qi, kiwi, qa, kiwi, ka, va)


def _page_copies(pt_ref, step, slot, layer, caches, bufs, sem, pages, page):
    group = bufs[0].shape[1]
    out = []
    for g in range(group):
        for j, p in enumerate(pages):
            pg = pt_ref[step * group + g, p]
            for ci, (cache, buf) in enumerate(zip(caches, bufs)):
                out.append(pltpu.make_async_copy(
                    cache.at[layer, pg], buf.at[slot, g, :, pl.ds(j * page, page)],
                    sem.at[slot, ci, g * len(pages) + j]))
    return out


def _prefetch_pages(pt_ref, layer, caches, bufs, sem, pages, page):
    s = pl.program_id(0)
    slot = s % 2

    @pl.when(s == 0)
    def _():
        for cp in _page_copies(pt_ref, 0, 0, layer, caches, bufs, sem, pages, page):
            cp.start()

    @pl.when(s + 1 < pl.num_programs(0))
    def _():
        for cp in _page_copies(pt_ref, s + 1, 1 - slot, layer, caches, bufs, sem, pages, page):
            cp.start()

    for cp in _page_copies(pt_ref, s, slot, layer, caches, bufs, sem, pages, page):
        cp.wait()
    return slot


def _dsa_sample_body(pt_ref, qi_ref, kiwi_ref, qa_ref, ka_ref, va_ref, cidx_ref, cak_ref, cav_ref,
                     o_ref, ki_buf, ka_buf, va_buf, sem, key_ref, bias_ref,
                     *, layer, n_pages, page, topk, n_new):
    group, tq = qi_ref.shape[0], qi_ref.shape[1]
    past = n_pages * page
    total = past + page
    rows_all = group * tq
    new = slice(past, total)
    slot = _prefetch_pages(pt_ref, layer, (cidx_ref, cak_ref, cav_ref), (ki_buf, ka_buf, va_buf),
                           sem, tuple(range(n_pages)), page)
    lane_q = lax.broadcasted_iota(I32, (tq, LANES), 1)
    k_pos = lax.broadcasted_iota(I32, (tq, total), 1)
    q_pos = past + lax.broadcasted_iota(I32, (tq, 1), 0)

    for g in range(group):
        kw_t = _pad_rows(kiwi_ref[g], page).T
        ki_buf[slot, g, :, new] = kw_t[0:IDX_DIM, :]
        ka_buf[slot, g, :, new] = _pad_rows(ka_ref[g], page).T
        va_buf[slot, g, :, new] = _pad_rows(va_ref[g], page).T

    for g in range(group):
        qi = qi_ref[g]
        wi = kiwi_ref[g][:, IDX_DIM:IDX_DIM + N_IDX_HEADS]
        ki_t = ki_buf[slot, g].astype(BF16)
        sc = jnp.zeros((tq, total), F32)
        for h in range(N_IDX_HEADS):
            d = _dot(qi[:, h * IDX_DIM:(h + 1) * IDX_DIM].astype(BF16), ki_t) * (IDX_DIM ** -0.5)
            sc = sc + jnp.maximum(d, 0.0) * wi[:, h:h + 1]
        key_ref[g * tq:(g + 1) * tq, :] = jnp.where(k_pos <= q_pos, _sort_key(sc), INT_MIN)

    def bit_step(b, thr):
        cand = thr + lax.shift_left(jnp.int32(1), 31 - b)
        cnt = jnp.sum(jnp.where(key_ref[...] >= cand, 1.0, 0.0), axis=1, keepdims=True)
        return jnp.where(cnt >= topk, cand, thr)

    thr = lax.fori_loop(0, 32, bit_step, jnp.full((rows_all, 1), INT_MIN, I32))

    keys = key_ref[...]
    eq_f = jnp.where(keys == thr, 1.0, 0.0)
    n_ties = topk - jnp.sum(jnp.where(keys > thr, 1.0, 0.0), axis=1, keepdims=True)
    real_row = lax.broadcasted_iota(I32, (rows_all, 1), 0) % tq < n_new
    surplus = jnp.where(real_row, jnp.sum(eq_f, axis=1, keepdims=True) - n_ties, 0.0)
    bias_ref[...] = jnp.where((keys >= thr) & (keys != INT_MIN), 0.0, MASKED_LOGIT)

    @pl.when(jnp.max(surplus) > 0.0)
    def _():
        r = lax.broadcasted_iota(I32, (page, page), 0)
        c = lax.broadcasted_iota(I32, (page, page), 1)
        prefix_incl = (r <= c).astype(BF16)
        seen = jnp.zeros((rows_all, 1), F32)
        for kt in range(n_pages + 1):
            lanes = slice(kt * page, (kt + 1) * page)
            k = key_ref[:, lanes]
            eq = k == thr
            e = jnp.where(eq, 1.0, 0.0)
            rank = _dot(e.astype(BF16), prefix_incl) + seen
            sel = ((k > thr) | (eq & (rank <= n_ties))) & (k != INT_MIN)
            bias_ref[:, lanes] = jnp.where(sel, 0.0, MASKED_LOGIT)
            seen = seen + jnp.sum(e, axis=1, keepdims=True)

    for g in range(group):
        qa = qa_ref[g]
        q_rows = []
        for h in range(N_HEADS_A):
            chunk = qa[:, (h // 2) * LANES:(h // 2 + 1) * LANES]
            q_rows.append(_half_select(chunk, h % 2, h // KV_GROUP, lane_q))
        q_stack = jnp.concatenate(q_rows, axis=0).astype(BF16)
        b = bias_ref[g * tq:(g + 1) * tq, :]
        lg = (_dot(q_stack, ka_buf[slot, g].astype(BF16)) * (HEAD_DIM ** -0.5)
              + jnp.concatenate([b] * N_HEADS_A, axis=0))
        p = jnp.exp(lg - jnp.max(lg, axis=1, keepdims=True))
        o = _dot_t(p.astype(BF16), va_buf[slot, g].astype(BF16)) / jnp.sum(p, axis=1, keepdims=True)
        for cc in range(N_HEADS_A // 2):
            parts = []
            for h in (2 * cc, 2 * cc + 1):
                parts.append(_half_select(o[h * tq:(h + 1) * tq], h // KV_GROUP, h % 2, lane_q))
            o_ref[g, :, cc * LANES:(cc + 1) * LANES] = parts[0] + parts[1]


def _dsa_sample(page_table, qi, kiwi, qa, ka, va, cache_idx_t, cache_ak_t, cache_av_t, layer, n_new):
    nb, n_pages = page_table.shape
    page = cache_idx_t.shape[3]
    tq = SAMPLE_PAD
    group = SAMPLE_GROUP
    total = (n_pages + 1) * page

    def q_spec(w):
        return pl.BlockSpec((group, tq, w), lambda s, pt: (s, 0, 0))

    hbm = pl.BlockSpec(memory_space=pl.ANY)
    grid_spec = pltpu.PrefetchScalarGridSpec(
        num_scalar_prefetch=1,
        grid=(nb // group,),
        in_specs=[q_spec(WIDTH_QI), q_spec(LANES), q_spec(WIDTH_A), q_spec(WIDTH_KV_A), q_spec(WIDTH_KV_A),
                  hbm, hbm, hbm],
        out_specs=q_spec(WIDTH_A),
        scratch_shapes=[pltpu.VMEM((2, group, IDX_DIM, total), F32),
                        pltpu.VMEM((2, group, WIDTH_KV_A, total), F32),
                        pltpu.VMEM((2, group, WIDTH_KV_A, total), F32),
                        pltpu.SemaphoreType.DMA((2, 3, group * n_pages)),
                        pltpu.VMEM((group * tq, total), I32),
                        pltpu.VMEM((group * tq, total), F32)],
    )
    topk = float(min(TOPK_MAX, (n_pages * page + n_new) // 4))
    return pl.pallas_call(
        functools.partial(_dsa_sample_body, layer=layer, n_pages=n_pages, page=page, topk=topk, n_new=n_new),
        grid_spec=grid_spec,
        out_shape=jax.ShapeDtypeStruct((nb, tq, WIDTH_A), F32),
        compiler_params=_params(1),
        name="dsa_sample",
    )(page_table, qi, kiwi, qa, ka, va, cache_idx_t, cache_ak_t, cache_av_t)


def _softplus(z):
    return jnp.maximum(z, 0.0) + jnp.log1p(jnp.exp(-jnp.abs(z)))


def _sb_core(qb, q_pos, kt_last, k_tile, v_tile, carry_ref, acc_ref, o_ref, tk):
    tq = qb.shape[0]
    lane_q = lax.broadcasted_iota(I32, (tq, LANES), 1)
    lane_k = lax.broadcasted_iota(I32, (tq, tk), 1)
    r = lax.broadcasted_iota(I32, (tk, tk), 0)
    c = lax.broadcasted_iota(I32, (tk, tk), 1)
    suffix_incl = (r >= c).astype(BF16)
    n_chunks = N_HEADS_B // 2
    q_pair = []
    for cc in range(n_chunks):
        chunk = qb[:, cc * LANES:(cc + 1) * LANES]
        halves = [jnp.where((lane_q // HEAD_DIM) == half, chunk, 0.0) for half in range(2)]
        q_pair.append(jnp.concatenate(halves, axis=0).astype(BF16))
    carry_ref[...] = jnp.zeros(carry_ref.shape, F32)
    acc_ref[...] = jnp.zeros(acc_ref.shape, F32)

    def cond(state):
        kt, worst = state
        return jnp.logical_and(kt >= 0, worst > F32_EXP_UNDERFLOW)

    def body(state):
        kt, _ = state
        strict = jnp.concatenate([kt * tk + lane_k < q_pos] * N_HEADS_B, axis=0)
        z = jnp.concatenate([_dot_t(q_pair[cc], k_tile(kt, cc).astype(BF16)) for cc in range(n_chunks)],
                            axis=0) * (HEAD_DIM ** -0.5)
        log_rem = jnp.where(strict, -_softplus(z), 0.0)
        rev = _split_dot(log_rem, suffix_incl)
        a = jnp.where(strict, jnp.exp(z + rev + carry_ref[...]), 0.0).astype(BF16)
        for cc in range(n_chunks):
            pair = slice(2 * cc * tq, 2 * (cc + 1) * tq)
            acc_ref[pair, :] += _dot(a[pair], v_tile(kt, cc).astype(BF16))
        carry = carry_ref[...] + rev[:, 0:1]
        carry_ref[...] = carry
        return kt - 1, jnp.max(carry)

    lax.while_loop(cond, body, (kt_last, jnp.float32(0.0)))
    for cc in range(n_chunks):
        o_ref[:, cc * LANES:(cc + 1) * LANES] = jnp.where(
            lane_q < HEAD_DIM, acc_ref[pl.ds(2 * cc * tq, tq), :], acc_ref[pl.ds((2 * cc + 1) * tq, tq), :])


def _sb_prompt_body(qb_ref, kb_ref, vb_ref, o_ref, carry_ref, acc_ref):
    tq = qb_ref.shape[0]
    i = pl.program_id(1)
    q_pos = i * tq + lax.broadcasted_iota(I32, (tq, 1), 0)

    def tile(ref):
        return lambda kt, cc: ref[pl.ds(pl.multiple_of(kt * tq, tq), tq), cc * LANES:(cc + 1) * LANES]

    _sb_core(qb_ref[...], q_pos, i, tile(kb_ref), tile(vb_ref), carry_ref, acc_ref, o_ref, tq)


def _sb_prompt(qb, kb, vb, batch, seq):
    tq = ATT_TILE
    nq = seq // tq
    q_spec = pl.BlockSpec((tq, WIDTH_B), lambda b, i: (b * nq + i, 0))
    k_spec = pl.BlockSpec((seq, WIDTH_B), lambda b, i: (b, 0))
    return pl.pallas_call(
        _sb_prompt_body,
        grid=(batch, nq),
        in_specs=[q_spec, k_spec, k_spec],
        out_specs=q_spec,
        out_shape=jax.ShapeDtypeStruct((batch * seq, WIDTH_B), F32),
        scratch_shapes=[pltpu.VMEM((N_HEADS_B * tq, 1), F32), pltpu.VMEM((N_HEADS_B * tq, LANES), F32)],
        compiler_params=_params(2),
        name="sb_prompt",
    )(qb, kb, vb)


def _sb_sample_body(pt_ref, qb_ref, kb_ref, vb_ref, ck_ref, cv_ref, o_ref,
                    k_buf, v_buf, sem, kf_buf, vf_buf, sem_f, carry_ref, acc_ref,
                    *, layer, n_pages, page):
    group, tq = qb_ref.shape[0], qb_ref.shape[1]
    past = n_pages * page
    eager = tuple(range(n_pages - 1, n_pages - 1 - SB_EAGER_PAGES, -1))
    slot = _prefetch_pages(pt_ref, layer, (ck_ref, cv_ref), (k_buf, v_buf), sem, eager, page)
    s = pl.program_id(0)
    n_chunks = N_HEADS_B // 2
    rows_g = N_HEADS_B * tq
    lane_q = lax.broadcasted_iota(I32, (tq, LANES), 1)
    r = lax.broadcasted_iota(I32, (page, page), 0)
    c = lax.broadcasted_iota(I32, (page, page), 1)
    suffix_incl = (r >= c).astype(BF16)
    strict_new = (lax.broadcasted_iota(I32, (group * rows_g, page), 1)
                  < lax.broadcasted_iota(I32, (group * rows_g, page), 0) % tq)

    def rows(cc):
        return slice(cc * LANES, (cc + 1) * LANES)

    def q_pairs(g):
        qb = qb_ref[g]
        out = []
        for cc in range(n_chunks):
            halves = [jnp.where((lane_q // HEAD_DIM) == half, qb[:, rows(cc)], 0.0) for half in range(2)]
            out.append(jnp.concatenate(halves, axis=0).astype(BF16))
        return out

    def visit(q, k_t, v_t, strict, carry, acc):
        z = jnp.concatenate([_dot(q[i][cc], k_t(i, cc).astype(BF16))
                             for i in range(len(q)) for cc in range(n_chunks)], axis=0) * (HEAD_DIM ** -0.5)
        log_rem = -_softplus(z)
        if strict is not None:
            log_rem = jnp.where(strict, log_rem, 0.0)
        rev = _split_dot(log_rem, suffix_incl)
        a = jnp.exp(z + rev + carry)
        if strict is not None:
            a = jnp.where(strict, a, 0.0)
        a = a.astype(BF16)
        new_acc = []
        for i in range(len(q)):
            for cc in range(n_chunks):
                j = i * n_chunks + cc
                new_acc.append(acc[j] + _dot_t(a[j * 2 * tq:(j + 1) * 2 * tq], v_t(i, cc).astype(BF16)))
        return carry + rev[:, 0:1], new_acc

    q_all = [q_pairs(g) for g in range(group)]
    carry = jnp.zeros((group * rows_g, 1), F32)
    acc = [jnp.zeros((2 * tq, LANES), F32) for _ in range(group * n_chunks)]
    carry, acc = visit(q_all, lambda g, cc: _pad_rows(kb_ref[g][:, rows(cc)], page).T,
                       lambda g, cc: _pad_rows(vb_ref[g][:, rows(cc)], page).T, strict_new, carry, acc)
    for j in range(SB_EAGER_PAGES):
        lanes = slice(j * page, (j + 1) * page)
        carry, acc = visit(q_all, lambda g, cc: k_buf[slot, g, rows(cc), lanes],
                           lambda g, cc: v_buf[slot, g, rows(cc), lanes], None, carry, acc)
    carry_ref[...] = carry
    for j in range(group * n_chunks):
        acc_ref[j] = acc[j]

    for g in range(group):
        seq = s * group + g
        g_rows = pl.ds(g * rows_g, rows_g)

        def cond(state):
            p, worst = state
            return jnp.logical_and(p >= 0, worst > F32_EXP_UNDERFLOW)

        def body(state):
            p, _ = state
            pg = pt_ref[seq, p]
            ck = pltpu.make_async_copy(ck_ref.at[layer, pg], kf_buf, sem_f.at[0])
            cv = pltpu.make_async_copy(cv_ref.at[layer, pg], vf_buf, sem_f.at[1])
            ck.start()
            cv.start()
            ck.wait()
            cv.wait()
            acc_g = [acc_ref[g * n_chunks + cc] for cc in range(n_chunks)]
            carry_g, acc_g = visit([q_all[g]], lambda i, cc: kf_buf[rows(cc), :], lambda i, cc: vf_buf[rows(cc), :],
                                   None, carry_ref[g_rows, :], acc_g)
            carry_ref[g_rows, :] = carry_g
            for cc in range(n_chunks):
                acc_ref[g * n_chunks + cc] = acc_g[cc]
            return p - 1, jnp.max(carry_g)

        lax.while_loop(cond, body, (jnp.int32(n_pages - 1 - SB_EAGER_PAGES), jnp.max(carry_ref[g_rows, :])))
        for cc in range(n_chunks):
            pair = acc_ref[g * n_chunks + cc]
            o_ref[g, :, rows(cc)] = jnp.where(lane_q < HEAD_DIM, pair[0:tq], pair[tq:2 * tq])


def _sb_sample(page_table, qb, kb, vb, cache_k_t, cache_v_t, layer):
    nb, n_pages = page_table.shape
    page = cache_k_t.shape[3]
    tq = SAMPLE_PAD
    group = SAMPLE_GROUP
    q_spec = pl.BlockSpec((group, tq, WIDTH_B), lambda s, pt: (s, 0, 0))
    hbm = pl.BlockSpec(memory_space=pl.ANY)
    grid_spec = pltpu.PrefetchScalarGridSpec(
        num_scalar_prefetch=1,
        grid=(nb // group,),
        in_specs=[q_spec, q_spec, q_spec, hbm, hbm],
        out_specs=q_spec,
        scratch_shapes=[pltpu.VMEM((2, group, WIDTH_B, SB_EAGER_PAGES * page), F32),
                        pltpu.VMEM((2, group, WIDTH_B, SB_EAGER_PAGES * page), F32),
                        pltpu.SemaphoreType.DMA((2, 2, group * SB_EAGER_PAGES)),
                        pltpu.VMEM((WIDTH_B, page), F32),
                        pltpu.VMEM((WIDTH_B, page), F32),
                        pltpu.SemaphoreType.DMA((2,)),
                        pltpu.VMEM((group * N_HEADS_B * tq, 1), F32),
                        pltpu.VMEM((group * N_HEADS_B // 2, 2 * tq, LANES), F32)],
    )
    return pl.pallas_call(
        functools.partial(_sb_sample_body, layer=layer, n_pages=n_pages, page=page),
        grid_spec=grid_spec,
        out_shape=jax.ShapeDtypeStruct((nb, tq, WIDTH_B), F32),
        compiler_params=_params(1),
        name="sb_sample",
    )(page_table, qb, kb, vb, cache_k_t, cache_v_t)


def _mix_body(h_ref, oa_ref, ob_ref, u_ref, halo_ref, woa_ref, wob_ref, woc_ref, wpool_ref, pscale_ref,
              o_ref, ext_a, ext_b, *, unit, tiles_per_seq, pos0, zero_first_halo):
    tm = h_ref.shape[0]
    i = pl.program_id(0)
    tile_in_seq = i % tiles_per_seq
    lo = 8 * unit
    base = 24 * unit
    n = base + tm
    u = u_ref[...]
    halo = halo_ref[...]
    if zero_first_halo:
        halo = jnp.where(tile_in_seq == 0, 0.0, halo)
    zeros = jnp.zeros((lo, WIDTH_C), F32)
    ext_a[pl.ds(0, lo), :] = zeros
    ext_b[pl.ds(0, lo), :] = zeros
    ext_a[pl.ds(lo, 16 * unit), :] = halo
    ext_a[pl.ds(base, tm), :] = u

    def doubled(src, shift):
        return src[pl.ds(lo, n - lo), :] + src[pl.ds(lo - shift * unit, n - lo), :]

    cur = 16 * unit
    s2 = doubled(ext_a, 1)
    ext_b[pl.ds(lo, n - lo), :] = s2
    s4 = doubled(ext_b, 2)
    ext_a[pl.ds(lo, n - lo), :] = s4
    s8 = doubled(ext_a, 4)
    ext_b[pl.ds(lo, n - lo), :] = s8
    s16 = doubled(ext_b, 8)
    lane = lax.broadcasted_iota(I32, (tm, WIDTH_C), 1)
    group = lane // POOL_GROUP_DIM
    sums = jnp.where(group == 0, s2[cur:], jnp.where(group == 1, s4[cur:],
                     jnp.where(group == 2, s8[cur:], s16[cur:])))
    window = jnp.where(group == 0, POOL_WINDOWS[0], jnp.where(group == 1, POOL_WINDOWS[1],
                       jnp.where(group == 2, POOL_WINDOWS[2], POOL_WINDOWS[3])))
    token = tile_in_seq * (tm // unit) + lax.broadcasted_iota(I32, (tm, WIDTH_C), 0) // unit
    cnt = jnp.minimum(window, pos0 + token + 1).astype(F32)
    pooled = (sums / cnt - u).astype(BF16)
    y = (_dot(pooled, wpool_ref[...]) * pscale_ref[...]).astype(BF16)
    mix = (_dot(oa_ref[...].astype(BF16), woa_ref[...]) + _dot(ob_ref[...].astype(BF16), wob_ref[...])
           + _dot(y, woc_ref[...]))
    o_ref[...] = h_ref[...] + mix


def _mix(h, oa, ob, u, halo, woa, wob, woc, wpool, pscale, *, unit, tiles_per_seq, pos0, prompt):
    rows, d = h.shape
    tm = DENSE_TILE
    n = rows // tm

    def row(w):
        return pl.BlockSpec((tm, w), lambda i: (i, 0))

    if prompt:
        per = tm // (16 * unit)
        halo_spec = pl.BlockSpec((16 * unit, WIDTH_C), lambda i: (jnp.maximum(i * per - 1, 0), 0))
    else:
        halo_spec = pl.BlockSpec((16 * unit, WIDTH_C), lambda i: (i, 0))
    consts = (woa, wob, woc, wpool, pscale)
    return pl.pallas_call(
        functools.partial(_mix_body, unit=unit, tiles_per_seq=tiles_per_seq, pos0=pos0, zero_first_halo=prompt),
        grid=(n,),
        in_specs=[row(d), row(WIDTH_A), row(WIDTH_B), row(WIDTH_C), halo_spec] + [_const_spec(a, 1) for a in consts],
        out_specs=row(d),
        out_shape=jax.ShapeDtypeStruct((rows, d), F32),
        scratch_shapes=[pltpu.VMEM((24 * unit + tm, WIDTH_C), F32), pltpu.VMEM((24 * unit + tm, WIDTH_C), F32)],
        compiler_params=_params(1),
        name="mix",
    )(h, oa, ob, u, halo, *consts)


def _ffn_body(h_ref, p_ref, pre_ref, gffn_ref, wup_ref, cw_ref, cb_ref, wdn_ref, gple_ref, wpg_ref, wpp_ref,
              o_ref, tail_ref, ext_ref, *, unit, tiles_per_seq):
    tm = h_ref.shape[0]
    halo = pre_ref.shape[0]
    dff = cw_ref.shape[1]
    i = pl.program_id(0)
    h = h_ref[...]
    f = _rms(h, gffn_ref[...]).astype(BF16)

    @pl.when(i % tiles_per_seq == 0)
    def _():
        ext_ref[pl.ds(0, halo), :] = pre_ref[...]

    h2 = h
    for c0 in range(0, dff, FFN_CHUNK):
        cols = slice(c0, min(c0 + FFN_CHUNK, dff))
        vcols = slice(dff + cols.start, dff + cols.stop)
        gate = _dot(f, wup_ref[:, cols])
        val = _dot(f, wup_ref[:, vcols])
        ext_ref[pl.ds(halo, tm), cols] = gate
        conv = cb_ref[:, cols] + ext_ref[pl.ds(halo - 2 * unit, tm), cols] * cw_ref[0:1, cols]
        conv = conv + ext_ref[pl.ds(halo - unit, tm), cols] * cw_ref[1:2, cols]
        conv = conv + gate * cw_ref[2:3, cols]
        act = (conv * _sigmoid(conv) * val).astype(BF16)
        h2 = h2 + _dot(act, wdn_ref[cols, :])
    tail = ext_ref[pl.ds(tm, halo), :]
    tail_ref[...] = tail
    ext_ref[pl.ds(0, halo), :] = tail
    gt = _sigmoid(_dot(_rms(h2, gple_ref[...]).astype(BF16), wpg_ref[...]))
    o_ref[...] = h2 + gt * _dot(p_ref[...].astype(BF16), wpp_ref[...])


def _ffn(h, p, pre, gffn, wup, cw, cb, wdn, gple, wpg, wpp, *, unit, tiles_per_seq, pre_per_seq):
    rows, d = h.shape
    tm = DENSE_TILE
    n = rows // tm
    halo = max(SUBLANES, (CONV_WIDTH - 1) * unit)
    dff = cw.shape[1]

    def row(w):
        return pl.BlockSpec((tm, w), lambda i: (i, 0))

    if pre_per_seq:
        pre_spec = pl.BlockSpec((halo, dff), lambda i: (i // tiles_per_seq, 0))
    else:
        pre_spec = pl.BlockSpec((halo, dff), lambda i: (0, 0))
    tail_spec = pl.BlockSpec((halo, dff), lambda i: (i // tiles_per_seq, 0))
    consts = (gffn, wup, cw, cb, wdn, gple, wpg, wpp)
    return pl.pallas_call(
        functools.partial(_ffn_body, unit=unit, tiles_per_seq=tiles_per_seq),
        grid=(n,),
        in_specs=[row(d), row(p.shape[1]), pre_spec] + [_const_spec(a, 1) for a in consts],
        out_specs=[row(d), tail_spec],
        out_shape=[jax.ShapeDtypeStruct((rows, d), F32),
                   jax.ShapeDtypeStruct((n // tiles_per_seq * halo, dff), F32)],
        scratch_shapes=[pltpu.VMEM((halo + tm, dff), F32)],
        compiler_params=_params(1),
        name="ffn",
    )(h, p, pre, *consts)


def _rope_tables(pos):
    half = HEAD_DIM // 2
    inv = ROPE_THETA ** (-jnp.arange(half, dtype=F32) / half)
    ang = pos.astype(F32)[:, None] * inv[None, :]
    cos = jnp.cos(ang)
    sin = jnp.sin(ang)
    cos_t = jnp.tile(cos, (1, LANES // half))
    sin_t = jnp.tile(jnp.concatenate([-sin, sin], axis=1), (1, LANES // HEAD_DIM))
    return cos_t, sin_t


def _matmul_weights(w_in, w_pool, w_out, w_up, w_down, w_ple_gate, w_ple_proj):
    starts = [0]
    for w in IN_SPLITS:
        starts.append(starts[-1] + w)
    w_perm = jnp.concatenate([w_in[:, :, starts[j]:starts[j + 1]] for j in _PERM_ORDER], axis=2)
    w_perm = jnp.pad(w_perm, ((0, 0), (0, 0), (0, _PERM_WIDTH - w_perm.shape[2])))
    groups = len(POOL_WINDOWS)
    eye = jnp.eye(groups, dtype=F32)
    wpool = (w_pool[:, :, :, None, :] * eye[None, :, None, :, None]).reshape(
        w_pool.shape[0], groups * POOL_GROUP_DIM, groups * POOL_GROUP_DIM)
    return tuple(_to_bf16(w) for w in (w_perm, wpool, w_out, w_up, w_down, w_ple_gate, w_ple_proj))


def _to_token_major(x, n_tok):
    nb, _, w = x.shape
    x = x[:, :n_tok].reshape(nb // SAMPLE_UNIT, SAMPLE_UNIT, n_tok, w)
    return x.transpose(0, 2, 1, 3).reshape(nb * n_tok, w)


def _state_token_major(x):
    nb, n_tok, w = x.shape
    return x.reshape(nb // SAMPLE_UNIT, SAMPLE_UNIT, n_tok, w).transpose(0, 2, 1, 3)


def _from_token_major(x, nb, n_tok):
    w = x.shape[1]
    x = x.reshape(nb // SAMPLE_UNIT, n_tok, SAMPLE_UNIT, w).transpose(0, 2, 1, 3)
    return x.reshape(nb, n_tok, w)


def kernel(x_prompt, x_sample, cache_a_k, cache_a_v, cache_idx_k, cache_b_k, cache_b_v, state_pool, state_conv, page_table, p_prompt, p_sample, g_attn, w_in, g_q_a, g_k_a, w_pool, pool_scale, w_out, g_ffn, w_up, conv_w, conv_b, w_down, g_ple, w_ple_gate, w_ple_proj):
    batch, seq, d = x_prompt.shape
    nb, n_new, _ = x_sample.shape
    depth, n_pool, page = cache_idx_k.shape[:3]
    n_pages = page_table.shape[1]
    past = n_pages * page
    dff = conv_w.shape[2]
    assert seq % ROW_TILE == 0 and seq % DENSE_TILE == 0 and seq % KEY_TILE == 0 and KEY_TILE % ATT_TILE == 0
    assert nb % SAMPLE_UNIT == 0 and nb % SAMPLE_GROUP == 0 and n_new <= SAMPLE_PAD and page == LANES
    assert (n_new * SAMPLE_UNIT) == DENSE_TILE and (nb * SAMPLE_PAD) % ROW_TILE == 0
    assert n_pages >= SB_EAGER_PAGES

    cos_p, sin_p = _rope_tables(jnp.arange(seq, dtype=I32))
    cos_s, sin_s = _rope_tables(past + jnp.arange(nb * SAMPLE_PAD, dtype=I32) % SAMPLE_PAD)
    cidx_t = cache_idx_k.transpose(0, 1, 3, 2)
    cak_t = cache_a_k.transpose(0, 1, 3, 4, 2).reshape(depth, n_pool, WIDTH_KV_A, page)
    cav_t = cache_a_v.transpose(0, 1, 3, 4, 2).reshape(depth, n_pool, WIDTH_KV_A, page)
    cbk_t = cache_b_k.transpose(0, 1, 3, 4, 2).reshape(depth, n_pool, WIDTH_B, page)
    cbv_t = cache_b_v.transpose(0, 1, 3, 4, 2).reshape(depth, n_pool, WIDTH_B, page)
    w_perm, wpool, wo, wup, wdn, wpg, wpp = _matmul_weights(
        w_in, w_pool, w_out, w_up, w_down, w_ple_gate, w_ple_proj)

    h_p = x_prompt.reshape(batch * seq, d)
    h_s = jnp.pad(x_sample, ((0, 0), (0, SAMPLE_PAD - n_new), (0, 0))).reshape(nb * SAMPLE_PAD, d)
    zero_pre = jnp.zeros((SUBLANES, dff), F32)
    rows_p, rows_s = [], []
    for l in range(depth):
        gq = jnp.tile(g_q_a[l], LANES // HEAD_DIM)[None, :]
        gk = jnp.tile(g_k_a[l], LANES // HEAD_DIM)[None, :]
        proj_w = (g_attn[l][None, :], w_perm[l])
        ffn_w = (g_ffn[l][None, :], wup[l], conv_w[l], conv_b[l][None, :], wdn[l], g_ple[l][None, :], wpg[l], wpp[l])
        mix_w = (wo[l, :WIDTH_A], wo[l, WIDTH_A:WIDTH_A + WIDTH_B], wo[l, WIDTH_A + WIDTH_B:],
                 wpool[l], pool_scale[l][None, :])

        qa, ka, va, qi, qb, kb, vb, u, kiwi = _in_proj(h_p, *proj_w, cos_p, sin_p, gq, gk, ROW_TILE)
        o_a = _dsa_prompt(qi, kiwi, qa, ka, va, batch, seq)
        o_b = _sb_prompt(qb, kb, vb, batch, seq)
        h1 = _mix(h_p, o_a, o_b, u, u, *mix_w, unit=1, tiles_per_seq=seq // DENSE_TILE, pos0=0, prompt=True)
        h_p, tail = _ffn(h1, p_prompt[l].reshape(batch * seq, -1), zero_pre, *ffn_w,
                         unit=1, tiles_per_seq=seq // DENSE_TILE, pre_per_seq=False)
        rows_p.append((
            ka.reshape(batch, seq, N_KV_A, HEAD_DIM), va.reshape(batch, seq, N_KV_A, HEAD_DIM),
            kiwi[:, :IDX_DIM].reshape(batch, seq, IDX_DIM),
            kb.reshape(batch, seq, N_HEADS_B, HEAD_DIM), vb.reshape(batch, seq, N_HEADS_B, HEAD_DIM),
            u.reshape(batch, seq, WIDTH_C)[:, seq - POOL_STATE:],
            tail.reshape(batch, SUBLANES, dff)[:, SUBLANES - (CONV_WIDTH - 1):]))

        qa, ka, va, qi, qb, kb, vb, u, kiwi = _in_proj(h_s, *proj_w, cos_s, sin_s, gq, gk, ROW_TILE)

        def nat(x):
            return x.reshape(nb, SAMPLE_PAD, x.shape[1])

        o_a = _dsa_sample(page_table, nat(qi), nat(kiwi), nat(qa), nat(ka), nat(va),
                          cidx_t, cak_t, cav_t, l, n_new)
        o_b = _sb_sample(page_table, nat(qb), nat(kb), nat(vb), cbk_t, cbv_t, l)
        u_nat = nat(u)[:, :n_new]
        pool_state = _state_token_major(state_pool[l])
        halo = jnp.pad(pool_state, ((0, 0), (1, 0), (0, 0), (0, 0))).reshape(-1, WIDTH_C)
        h1 = _mix(_to_token_major(nat(h_s), n_new), _to_token_major(o_a, n_new), _to_token_major(o_b, n_new),
                  _to_token_major(nat(u), n_new), halo, *mix_w,
                  unit=SAMPLE_UNIT, tiles_per_seq=1, pos0=past, prompt=False)
        pre = _state_token_major(state_conv[l]).reshape(-1, dff)
        h3, tail = _ffn(h1, _to_token_major(p_sample[l], n_new), pre, *ffn_w,
                        unit=SAMPLE_UNIT, tiles_per_seq=1, pre_per_seq=True)
        h_s_nat = _from_token_major(h3, nb, n_new)
        h_s = jnp.pad(h_s_nat, ((0, 0), (0, SAMPLE_PAD - n_new), (0, 0))).reshape(nb * SAMPLE_PAD, d)
        conv_state = tail.reshape(nb // SAMPLE_UNIT, CONV_WIDTH - 1, SAMPLE_UNIT, dff)
        conv_state = conv_state.transpose(0, 2, 1, 3).reshape(nb, CONV_WIDTH - 1, dff)
        rows_s.append((
            nat(ka)[:, :n_new].reshape(nb, n_new, N_KV_A, HEAD_DIM),
            nat(va)[:, :n_new].reshape(nb, n_new, N_KV_A, HEAD_DIM),
            nat(kiwi)[:, :n_new, :IDX_DIM],
            nat(kb)[:, :n_new].reshape(nb, n_new, N_HEADS_B, HEAD_DIM),
            nat(vb)[:, :n_new].reshape(nb, n_new, N_HEADS_B, HEAD_DIM),
            jnp.concatenate([state_pool[l], u_nat], axis=1)[:, n_new:],
            conv_state))

    def stack(rows, j):
        return jnp.stack([r[j] for r in rows], axis=0)

    y_p = h_p.reshape(batch, seq, d)
    y_s = h_s.reshape(nb, SAMPLE_PAD, d)[:, :n_new]
    return (y_p, y_s) + tuple(stack(rows_p, j) for j in range(7)) + tuple(stack(rows_s, j) for j in range(7))
```
